```python
import math
import jax, jax.numpy as jnp
from jax import lax
import numpy as np

D_MODEL = 1024
BATCH = 8
SEQ = 2048
DEPTH = 4
DEC_BATCH = 128
DEC_SEQ = 1
PAST_LEN = 2048
PAGE_SIZE = 128

N_EVEN = (DEPTH + 1) // 2
N_ODD = DEPTH // 2
RMS_EPS = 1e-6
NEG = -1e30

HEAD_DIM = 64
A_HEADS = 8
A_WIDTH = A_HEADS * HEAD_DIM
A_BRANCHES = ((128, 1), (512, 4), (2048, 16))
A_MAX_WINDOW = 2048
A_KEYS = 128
A_BLOCK = 128

POOL_WINDOWS = (2, 4, 8, 16)
POOL_GROUPS = 4
POOL_WIDTH = D_MODEL - A_WIDTH
POOL_GROUP_DIM = POOL_WIDTH // POOL_GROUPS
POOL_HIST = max(POOL_WINDOWS) - 1
EVEN_IN = 3 * A_WIDTH + POOL_WIDTH
EVEN_MIX = A_WIDTH + POOL_WIDTH

GDN_HEADS = 8
GDN_DK = 128
GDN_DV = 128
GDN_QK = GDN_HEADS * GDN_DK
GDN_VW = GDN_HEADS * GDN_DV
GDN_CONV_CH = 2 * GDN_QK + GDN_VW
CONV_WIDTH = 4
GDN_CHUNK = 64
ODD_IN = GDN_CONV_CH + GDN_VW + 2 * GDN_HEADS

FFN_HIDDEN = -(-8 * D_MODEL // (3 * 256)) * 256

kernel_name = 'hybrid_dilated_pool_gdn_decoder_step'

F32 = jnp.float32


def _rmsnorm(x, g):
    xf = x.astype(F32)
    y = xf * lax.rsqrt(jnp.mean(xf * xf, axis=-1, keepdims=True) + RMS_EPS)
    return (y * g.astype(F32)).astype(x.dtype)


def _l2norm(x):
    return x * lax.rsqrt(jnp.sum(x * x, axis=-1, keepdims=True) + 1e-6)


def _alibi_slopes(n):
    return jnp.exp2(-8.0 * (jnp.arange(n, dtype=F32) + 1.0) / n)


def _swiglu(h, wg, wu, wd):
    return (jax.nn.silu(h @ wg) * (h @ wu)) @ wd


def _dilated_branch_prompt(q, k, v, dil, slopes):
    b, s, h, dh = q.shape
    span = dil * A_BLOCK
    s_pad = -(-s // span) * span
    L = s_pad // dil
    nb = L // A_BLOCK

    def res_blocks(a):
        a = jnp.pad(a, ((0, 0), (0, s_pad - s), (0, 0), (0, 0)))
        a = a.reshape(b, L, dil, h, dh).transpose(0, 2, 1, 3, 4)
        return a.reshape(b * dil, nb, A_BLOCK, h, dh)

    def with_prev(a):
        prev = jnp.pad(a, ((0, 0), (1, 0), (0, 0), (0, 0), (0, 0)))[:, :-1]
        return jnp.concatenate([prev, a], axis=2)

    qb = res_blocks(q).astype(F32)
    kk = with_prev(res_blocks(k)).astype(F32)
    vv = with_prev(res_blocks(v)).astype(F32)
    scores = jnp.einsum('znqhd,znkhd->znhqk', qb, kk) * (HEAD_DIM ** -0.5)
    rel = jnp.arange(A_BLOCK)[:, None] + A_BLOCK - jnp.arange(2 * A_BLOCK)[None, :]
    band = (rel >= 0) & (rel <= A_KEYS)
    key_idx = jnp.arange(nb)[:, None] * A_BLOCK - A_BLOCK + jnp.arange(2 * A_BLOCK)[None, :]
    mask = band[None] & (key_idx >= 0)[:, None, :]
    bias = -slopes[:, None, None] * (rel * dil).astype(F32)[None]
    scores = jnp.where(mask[None, :, None], scores + bias[None, None], NEG)
    m = jnp.max(scores, axis=-1, keepdims=True)
    p = jnp.exp(scores - m)
    l = jnp.sum(p, axis=-1, keepdims=True)
    o = jnp.einsum('znhqk,znkhd->znhqd', p, vv) / l
    lse = (m + jnp.log(l))[..., 0]
    o = o.transpose(0, 1, 3, 2, 4).reshape(b, dil, L, h, dh).transpose(0, 2, 1, 3, 4)
    o = o.reshape(b, s_pad, h, dh)[:, :s]
    lse = lse.transpose(0, 1, 3, 2).reshape(b, dil, L, h).transpose(0, 2, 1, 3)
    lse = lse.reshape(b, s_pad, h)[:, :s]
    return o, lse


def _dilated_branch_decode(q, kx, vx, n_hist, dil, slopes):
    t = q.shape[1]
    j = jnp.arange(A_KEYS + 1)
    idx = n_hist + jnp.arange(t)[:, None] - dil * j[None, :]
    valid = idx >= 0
    idx = jnp.maximum(idx, 0)
    kg = kx[:, idx].astype(F32)
    vg = vx[:, idx].astype(F32)
    scores = jnp.einsum('bthd,btjhd->bthj', q.astype(F32), kg) * (HEAD_DIM ** -0.5)
    scores = scores - slopes[:, None] * (dil * j).astype(F32)[None, :]
    scores = jnp.where(valid[None, :, None, :], scores, NEG)
    m = jnp.max(scores, axis=-1, keepdims=True)
    p = jnp.exp(scores - m)
    l = jnp.sum(p, axis=-1, keepdims=True)
    o = jnp.einsum('bthj,btjhd->bthd', p, vg) / l
    lse = (m + jnp.log(l))[..., 0]
    return o, lse


def _mix_dilations(res):
    o = jnp.stack([r[0] for r in res], axis=0)
    lse = jnp.stack([r[1] for r in res], axis=0)
    w = jax.nn.softmax(lse, axis=0)
    return jnp.einsum('rbth,rbthd->bthd', w, o)


def _pool_mixer(u, hist, pos0, w_pool, scale):
    b, t, c = u.shape
    ux = jnp.concatenate([hist, u], axis=1)
    uf = ux.astype(F32)
    cs = jnp.concatenate([jnp.zeros((b, 1, c), F32), jnp.cumsum(uf, axis=1)], axis=1)
    end = cs[:, POOL_HIST + 1:]
    cur = uf[:, POOL_HIST:]
    pos = pos0 + jnp.arange(t)
    outs = []
    for g, w in enumerate(POOL_WINDOWS):
        sl = slice(g * POOL_GROUP_DIM, (g + 1) * POOL_GROUP_DIM)
        start = cs[:, POOL_HIST + 1 - w:POOL_HIST + 1 - w + t, sl]
        cnt = jnp.minimum(pos + 1, w).astype(F32)[None, :, None]
        pooled = (end[..., sl] - start) / cnt - cur[..., sl]
        outs.append(jnp.einsum('btc,cd->btd', pooled, w_pool[g].astype(F32)))
    y = jnp.concatenate(outs, axis=-1) * scale.astype(F32)
    return y, ux[:, -POOL_HIST:]


def _even_mixer(h, k_buf, v_buf, pool_hist, pos0, w_in, w_out, w_pool, scale, slopes):
    b, t, _ = h.shape
    z = h @ w_in
    q = z[..., :A_WIDTH].reshape(b, t, A_HEADS, HEAD_DIM)
    k = z[..., A_WIDTH:2 * A_WIDTH].reshape(b, t, A_HEADS, HEAD_DIM)
    v = z[..., 2 * A_WIDTH:3 * A_WIDTH].reshape(b, t, A_HEADS, HEAD_DIM)
    u = z[..., 3 * A_WIDTH:]
    if k_buf is None:
        res = [_dilated_branch_prompt(q, k, v, dil, slopes) for _, dil in A_BRANCHES]
        keep = min(A_MAX_WINDOW, t)
        new_k, new_v = k[:, t - keep:], v[:, t - keep:]
    else:
        n_hist = k_buf.shape[1]
        kx = jnp.concatenate([k_buf, k], axis=1)
        vx = jnp.concatenate([v_buf, v], axis=1)
        res = [_dilated_branch_decode(q, kx, vx, n_hist, dil, slopes) for _, dil in A_BRANCHES]
        keep = min(A_MAX_WINDOW, n_hist + t)
        new_k, new_v = kx[:, n_hist + t - keep:], vx[:, n_hist + t - keep:]
    attn = _mix_dilations(res).reshape(b, t, A_WIDTH)
    pooled, new_pool = _pool_mixer(u, pool_hist, pos0, w_pool, scale)
    y = jnp.concatenate([attn, pooled], axis=-1).astype(h.dtype) @ w_out
    return y, new_k, new_v, new_pool


def _short_conv(xr, hist, w):
    xx = jnp.concatenate([hist, xr], axis=1)
    y = lax.conv_general_dilated(xx, w[:, None, :], window_strides=(1,), padding='VALID',
                                 dimension_numbers=('NWC', 'WIO', 'NWC'),
                                 feature_group_count=xr.shape[-1])
    return jax.nn.silu(y), xx[:, -(CONV_WIDTH - 1):]


def _gated_delta_chunked(q, k, v, g, beta, s0):
    b, t, h, dk = q.shape
    dv = v.shape[-1]
    c = min(GDN_CHUNK, t)
    tp = -(-t // c) * c
    n = tp // c

    def chunks(a):
        a = jnp.pad(a, ((0, 0), (0, tp - t)) + ((0, 0),) * (a.ndim - 2))
        a = a.reshape((b, n, c) + a.shape[2:])
        return a.transpose(1, 0, 3, 2, 4) if a.ndim == 5 else a.transpose(1, 0, 3, 2)

    qc, kc, vc = chunks(q), chunks(k), chunks(v)
    gc = jnp.cumsum(chunks(g), axis=-1)
    bc = chunks(beta)
    kb = kc * bc[..., None]
    vb = vc * bc[..., None]
    ar = jnp.arange(c)
    tril = ar[:, None] >= ar[None, :]
    strict = ar[:, None] > ar[None, :]
    diff = gc[..., :, None] - gc[..., None, :]
    decay = jnp.where(tril, jnp.exp(jnp.where(tril, diff, 0.0)), 0.0)
    lmat = jnp.where(strict, jnp.einsum('nbhid,nbhjd->nbhij', kb, kc) * decay, 0.0)
    eye = jnp.eye(c, dtype=F32)
    tinv = lax.linalg.triangular_solve(eye + lmat, jnp.broadcast_to(eye, lmat.shape),
                                       left_side=True, lower=True)
    u = tinv @ vb
    w = tinv @ (kb * jnp.exp(gc)[..., None])
    a_intra = jnp.where(tril, jnp.einsum('nbhid,nbhjd->nbhij', qc, kc) * decay, 0.0)
    q_dec = qc * jnp.exp(gc)[..., None]
    k_dec = kc * jnp.exp(gc[..., -1:] - gc)[..., None]
    g_last = jnp.exp(gc[..., -1])

    def step(s, xs):
        qd, kd, ui, wi, ai, gl = xs
        v_new = ui - wi @ s
        o = qd @ s + ai @ v_new
        s = s * gl[..., None, None] + jnp.einsum('bhck,bhcv->bhkv', kd, v_new)
        return s, o

    s_fin, o = lax.scan(step, s0, (q_dec, k_dec, u, w, a_intra, g_last))
    o = o.transpose(1, 0, 3, 2, 4).reshape(b, tp, h, dv)[:, :t]
    return o, s_fin


def _gdn_mixer(h, conv_hist, s0, w_in, w_conv, a_log, dt_bias, norm_w, w_out):
    b, t, _ = h.shape
    z = h @ w_in
    qkv = z[..., :GDN_CONV_CH]
    gate = z[..., GDN_CONV_CH:GDN_CONV_CH + GDN_VW]
    b_raw = z[..., GDN_CONV_CH + GDN_VW:GDN_CONV_CH + GDN_VW + GDN_HEADS]
    a_raw = z[..., GDN_CONV_CH + GDN_VW + GDN_HEADS:]
    qkv, new_conv = _short_conv(qkv, conv_hist, w_conv)
    qkv = qkv.astype(F32)
    q = _l2norm(qkv[..., :GDN_QK].reshape(b, t, GDN_HEADS, GDN_DK)) * (GDN_DK ** -0.5)
    k = _l2norm(qkv[..., GDN_QK:2 * GDN_QK].reshape(b, t, GDN_HEADS, GDN_DK))
    v = qkv[..., 2 * GDN_QK:].reshape(b, t, GDN_HEADS, GDN_DV)
    beta = jax.nn.sigmoid(b_raw.astype(F32))
    g = -jnp.exp(a_log.astype(F32)) * jax.nn.softplus(a_raw.astype(F32) + dt_bias.astype(F32))
    o, s_new = _gated_delta_chunked(q, k, v, g, beta, s0.astype(F32))
    o = o * lax.rsqrt(jnp.mean(o * o, axis=-1, keepdims=True) + RMS_EPS) * norm_w.astype(F32)
    o = o.reshape(b, t, GDN_VW) * jax.nn.silu(gate.astype(F32))
    y = o.astype(h.dtype) @ w_out
    return y, new_conv, s_new.astype(s0.dtype)


def setup_inputs(seed: int = 0) -> dict:
    key = jax.random.key(seed)
    ks = jax.random.split(key, 24)

    def nrm(k, shape, scale):
        return jax.random.normal(k, shape, F32) * scale

    n_hist_a = min(A_MAX_WINDOW, PAST_LEN)
    dt = jnp.exp(jax.random.uniform(ks[18], (N_ODD, GDN_HEADS), F32, math.log(1e-3), math.log(1e-1)))
    return {
        'x_prompt': nrm(ks[0], (BATCH, SEQ, D_MODEL), 1.0),
        'x_sample': nrm(ks[1], (DEC_BATCH, DEC_SEQ, D_MODEL), 1.0),
        'state_a_k': nrm(ks[2], (N_EVEN, DEC_BATCH, n_hist_a, A_HEADS, HEAD_DIM), 1.0),
        'state_a_v': nrm(ks[3], (N_EVEN, DEC_BATCH, n_hist_a, A_HEADS, HEAD_DIM), 1.0),
        'state_pool': nrm(ks[4], (N_EVEN, DEC_BATCH, POOL_HIST, POOL_WIDTH), 1.0),
        'state_conv': nrm(ks[5], (N_ODD, DEC_BATCH, CONV_WIDTH - 1, GDN_CONV_CH), 1.0),
        'state_gdn': nrm(ks[6], (N_ODD, DEC_BATCH, GDN_HEADS, GDN_DK, GDN_DV), 0.05),
        'norm_mix': 1.0 + nrm(ks[7], (DEPTH, D_MODEL), 0.02),
        'norm_ffn': 1.0 + nrm(ks[8], (DEPTH, D_MODEL), 0.02),
        'w_in_a': nrm(ks[9], (N_EVEN, D_MODEL, EVEN_IN), D_MODEL ** -0.5),
        'w_out_a': nrm(ks[10], (N_EVEN, EVEN_MIX, D_MODEL), EVEN_MIX ** -0.5),
        'w_pool': nrm(ks[11], (N_EVEN, POOL_GROUPS, POOL_GROUP_DIM, POOL_GROUP_DIM), POOL_GROUP_DIM ** -0.5),
        'pool_scale': 1.0 + nrm(ks[12], (N_EVEN, POOL_WIDTH), 0.02),
        'w_in_c': nrm(ks[13], (N_ODD, D_MODEL, ODD_IN), D_MODEL ** -0.5),
        'w_conv_c': nrm(ks[14], (N_ODD, CONV_WIDTH, GDN_CONV_CH), CONV_WIDTH ** -0.5),
        'a_log_c': jnp.log(jax.random.uniform(ks[15], (N_ODD, GDN_HEADS), F32, 1.0, 16.0)),
        'dt_bias_c': dt + jnp.log(-jnp.expm1(-dt)),
        'norm_out_c': 1.0 + nrm(ks[16], (N_ODD, GDN_DV), 0.02),
        'w_out_c': nrm(ks[17], (N_ODD, GDN_VW, D_MODEL), GDN_VW ** -0.5),
        'w_ffn_gate': nrm(ks[19], (DEPTH, D_MODEL, FFN_HIDDEN), D_MODEL ** -0.5),
        'w_ffn_up': nrm(ks[20], (DEPTH, D_MODEL, FFN_HIDDEN), D_MODEL ** -0.5),
        'w_ffn_down': nrm(ks[21], (DEPTH, FFN_HIDDEN, D_MODEL), FFN_HIDDEN ** -0.5),
        'norm_final': 1.0 + nrm(ks[22], (D_MODEL,), 0.02),
    }


def reference(x_prompt, x_sample, state_a_k, state_a_v, state_pool, state_conv, state_gdn,
              norm_mix, norm_ffn, w_in_a, w_out_a, w_pool, pool_scale,
              w_in_c, w_conv_c, a_log_c, dt_bias_c, norm_out_c, w_out_c,
              w_ffn_gate, w_ffn_up, w_ffn_down, norm_final):
    slopes = _alibi_slopes(A_HEADS)
    xp, xs = x_prompt, x_sample
    bp, bs = xp.shape[0], xs.shape[0]
    ak_p, av_p, pl_p, cv_p, gd_p = [], [], [], [], []
    ak_s, av_s, pl_s, cv_s, gd_s = [], [], [], [], []
    for layer in range(DEPTH):
        i = layer // 2
        hp = _rmsnorm(xp, norm_mix[layer])
        hs = _rmsnorm(xs, norm_mix[layer])
        if layer % 2 == 0:
            yp, kp, vp, pp = _even_mixer(hp, None, None, jnp.zeros((bp, POOL_HIST, POOL_WIDTH), hp.dtype), 0,
                                         w_in_a[i], w_out_a[i], w_pool[i], pool_scale[i], slopes)
            ys, ks_, vs, ps = _even_mixer(hs, state_a_k[i], state_a_v[i], state_pool[i], PAST_LEN,
                                          w_in_a[i], w_out_a[i], w_pool[i], pool_scale[i], slopes)
            ak_p.append(kp); av_p.append(vp); pl_p.append(pp)
            ak_s.append(ks_); av_s.append(vs); pl_s.append(ps)
        else:
            yp, cp, sp = _gdn_mixer(hp, jnp.zeros((bp, CONV_WIDTH - 1, GDN_CONV_CH), hp.dtype),
                                    jnp.zeros((bp, GDN_HEADS, GDN_DK, GDN_DV), state_gdn.dtype),
                                    w_in_c[i], w_conv_c[i], a_log_c[i], dt_bias_c[i], norm_out_c[i], w_out_c[i])
            ys, cs_, ss = _gdn_mixer(hs, state_conv[i], state_gdn[i],
                                     w_in_c[i], w_conv_c[i], a_log_c[i], dt_bias_c[i], norm_out_c[i], w_out_c[i])
            cv_p.append(cp); gd_p.append(sp)
            cv_s.append(cs_); gd_s.append(ss)
        xp = xp + yp
        xs = xs + ys
        xp = xp + _swiglu(_rmsnorm(xp, norm_ffn[layer]), w_ffn_gate[layer], w_ffn_up[layer], w_ffn_down[layer])
        xs = xs + _swiglu(_rmsnorm(xs, norm_ffn[layer]), w_ffn_gate[layer], w_ffn_up[layer], w_ffn_down[layer])
    y_prompt = _rmsnorm(xp, norm_final)
    y_sample = _rmsnorm(xs, norm_final)
    a_k_prompt, a_v_prompt = jnp.stack(ak_p), jnp.stack(av_p)
    pool_prompt, conv_prompt, gdn_prompt = jnp.stack(pl_p), jnp.stack(cv_p), jnp.stack(gd_p)
    a_k_sample, a_v_sample = jnp.stack(ak_s), jnp.stack(av_s)
    pool_sample, conv_sample, gdn_sample = jnp.stack(pl_s), jnp.stack(cv_s), jnp.stack(gd_s)
    return (y_prompt, y_sample, a_k_prompt, a_v_prompt, pool_prompt, conv_prompt, gdn_prompt,
            a_k_sample, a_v_sample, pool_sample, conv_sample, gdn_sample)
```

```python
import functools

import jax
import jax.numpy as jnp
from jax import lax
from jax.experimental import pallas as pl
from jax.experimental.pallas import tpu as pltpu

F32 = jnp.float32
BF16 = jnp.bfloat16

D_MODEL = 1024
RMS_EPS = 1e-6
NEG = -1e30

HEAD_DIM = 64
A_HEADS = 8
A_WIDTH = A_HEADS * HEAD_DIM
A_DILS = (1, 4, 16)
A_KEYS = 128
A_BLOCK = 128
A_SCALE = HEAD_DIM ** -0.5
N_PAIRS = A_HEADS // 2

POOL_WINDOWS = (2, 4, 8, 16)
POOL_WIDTH = 512
POOL_GROUP_DIM = 128
POOL_HIST = 15
POOL_PAD = 16

GDN_HEADS = 8
GDN_DK = 128
GDN_QK = GDN_HEADS * GDN_DK
GDN_VW = GDN_HEADS * GDN_DK
GDN_CONV_CH = 3 * GDN_QK
CONV_WIDTH = 4
GDN_CHUNK = 64
GDN_ROWS = 256
GDN_NC = GDN_ROWS // GDN_CHUNK
HALO = 8

FFN_HIDDEN = 2816
FFN_CHUNK = 256
LANE = 128

VMEM_LIMIT = 56 * 1024 * 1024


def _cparams(sem):
    return pltpu.CompilerParams(dimension_semantics=sem, vmem_limit_bytes=VMEM_LIMIT)


def _resident(shape):
    nd = len(shape)
    return pl.BlockSpec(shape, lambda *_: (0,) * nd, pipeline_mode=pl.Buffered(1))


def _rms(x, g):
    ms = jnp.mean(x * x, axis=-1, keepdims=True)
    return x * lax.rsqrt(ms + RMS_EPS) * g


def _silu(x):
    return x * jax.nn.sigmoid(x)


def _softplus(x):
    return jnp.maximum(x, 0.0) + jnp.log1p(jnp.exp(-jnp.abs(x)))


def _dot(a, b):
    return jnp.dot(a.astype(BF16), b.astype(BF16), preferred_element_type=F32)


def _dot_nt(a, b):
    return lax.dot_general(a.astype(BF16), b.astype(BF16), (((1,), (1,)), ((), ())),
                           preferred_element_type=F32)


def _dot_tn(a, b):
    return lax.dot_general(a.astype(BF16), b.astype(BF16), (((0,), (0,)), ((), ())),
                           preferred_element_type=F32)


def _split(x):
    hi = x.astype(BF16)
    lo = (x - hi.astype(F32)).astype(BF16)
    return hi, lo


def _dot3(a, b):
    ah, al = _split(a)
    bh, bl = _split(b)
    out = jnp.dot(ah, bh, preferred_element_type=F32)
    out = out + jnp.dot(al, bh, preferred_element_type=F32)
    return out + jnp.dot(ah, bl, preferred_element_type=F32)


def _norm_proj_kernel(n_out, x_ref, g_ref, *refs):
    w_refs, o_refs = refs[:n_out], refs[n_out:]
    h = _rms(x_ref[...], g_ref[...]).astype(BF16)
    for w_ref, o_ref in zip(w_refs, o_refs):
        n = w_ref.shape[1]
        for c0 in range(0, n, 512):
            c1 = min(c0 + 512, n)
            o_ref[:, c0:c1] = jnp.dot(h, w_ref[:, c0:c1], preferred_element_type=F32)


def _norm_proj(x, g, ws, tm):
    m, d = x.shape
    n_out = len(ws)
    return pl.pallas_call(
        functools.partial(_norm_proj_kernel, n_out),
        grid=(m // tm,),
        in_specs=[pl.BlockSpec((tm, d), lambda i: (i, 0)), _resident((1, d))]
        + [_resident(w.shape) for w in ws],
        out_specs=[pl.BlockSpec((tm, w.shape[1]), lambda i: (i, 0)) for w in ws],
        out_shape=[jax.ShapeDtypeStruct((m, w.shape[1]), F32) for w in ws],
        compiler_params=_cparams(("parallel",)),
        name="norm_proj",
    )(x, g, *ws)


def _mix_ffn_kernel(n_mix, final, x_ref, *refs):
    a_refs = refs[:n_mix]
    wo_refs = refs[n_mix:2 * n_mix]
    g_ref, wg_ref, wu_ref, wd_ref = refs[2 * n_mix:2 * n_mix + 4]
    rest = refs[2 * n_mix + 4:]
    if final:
        gf_ref, o_ref, act_ref = rest
    else:
        o_ref, act_ref = rest
    x1 = x_ref[...]
    for a_ref, wo_ref in zip(a_refs, wo_refs):
        x1 = x1 + jnp.dot(a_ref[...].astype(BF16), wo_ref[...], preferred_element_type=F32)
    h = _rms(x1, g_ref[...]).astype(BF16)
    for c0 in range(0, FFN_HIDDEN, FFN_CHUNK):
        gate = jnp.dot(h, wg_ref[:, c0:c0 + FFN_CHUNK], preferred_element_type=F32)
        up = jnp.dot(h, wu_ref[:, c0:c0 + FFN_CHUNK], preferred_element_type=F32)
        act_ref[:, c0:c0 + FFN_CHUNK] = (_silu(gate) * up).astype(BF16)
    out = x1 + jnp.dot(act_ref[...], wd_ref[...], preferred_element_type=F32)
    if final:
        out = _rms(out, gf_ref[...])
    o_ref[...] = out


def _mix_ffn(x, mixes, wos, g, wg, wu, wd, g_final, tm):
    m, d = x.shape
    n_mix = len(mixes)
    final = g_final is not None
    in_specs = [pl.BlockSpec((tm, d), lambda i: (i, 0))]
    in_specs += [pl.BlockSpec((tm, a.shape[1]), lambda i: (i, 0)) for a in mixes]
    in_specs += [_resident(w.shape) for w in wos]
    in_specs += [_resident(g.shape), _resident(wg.shape), _resident(wu.shape), _resident(wd.shape)]
    args = [x, *mixes, *wos, g, wg, wu, wd]
    if final:
        in_specs.append(_resident(g_final.shape))
        args.append(g_final)
    return pl.pallas_call(
        functools.partial(_mix_ffn_kernel, n_mix, final),
        grid=(m // tm,),
        in_specs=in_specs,
        out_specs=pl.BlockSpec((tm, d), lambda i: (i, 0)),
        out_shape=jax.ShapeDtypeStruct((m, d), F32),
        scratch_shapes=[pltpu.VMEM((tm, FFN_HIDDEN), BF16)],
        compiler_params=_cparams(("parallel",)),
        name="mix_ffn",
    )(*args)


def _attn_bias():
    rel = jnp.arange(A_BLOCK)[:, None] + A_BLOCK - jnp.arange(2 * A_BLOCK)[None, :]
    band = (rel >= 0) & (rel <= A_KEYS)
    slopes = jnp.exp2(-8.0 * (jnp.arange(A_HEADS, dtype=F32) + 1.0) / A_HEADS)
    dils = jnp.asarray(A_DILS, F32)
    bias = -slopes[None, :, None, None] * (rel.astype(F32)[None, None] * dils[:, None, None, None])
    bias = jnp.where(band[None, None], bias, NEG)
    bias = bias.reshape(len(A_DILS), N_PAIRS, 2, A_BLOCK, 2 * A_BLOCK)
    return bias.transpose(1, 0, 2, 3, 4)


def _attn_scores(q, kk, vv, bias_ref, d_idx, first):
    lane = lax.broadcasted_iota(jnp.int32, (A_BLOCK, LANE), 1)
    lo = lane < HEAD_DIM
    kb = kk.astype(BF16)
    vb = vv.astype(BF16)
    stats = []
    for h in range(2):
        qm = jnp.where(lo if h == 0 else jnp.logical_not(lo), q * A_SCALE, 0.0)
        s = _dot_nt(qm, kb)
        if first:
            s = s + bias_ref[d_idx, h, :, A_BLOCK:]
        else:
            s = s + bias_ref[d_idx, h]
        m = jnp.max(s, axis=-1, keepdims=True)
        p = jnp.exp(s - m)
        l = jnp.sum(p, axis=-1, keepdims=True)
        pv = jnp.dot(p.astype(BF16), vb, preferred_element_type=F32)
        stats.append((m, l, pv))
    m = jnp.where(lo, stats[0][0], stats[1][0])
    l = jnp.where(lo, stats[0][1], stats[1][1])
    acc = jnp.where(lo, stats[0][2], stats[1][2])
    return m, l, acc


def _even_prompt_kernel(seq, q_ref, k_ref, v_ref, u_ref, bias_ref, wp_ref, sc_ref,
                        attn_ref, pool_ref, m_scr, l_scr, acc_scr, pad_scr):
    def rows(dil, start):
        if dil == 1:
            return pl.ds(start, A_BLOCK)
        return pl.ds(start, A_BLOCK, stride=dil)

    def block(d_idx, dil, start, first):
        cur = rows(dil, start)
        q = q_ref[cur, :]
        if first:
            kk, vv = k_ref[cur, :], v_ref[cur, :]
        elif dil == 1:
            win = pl.ds(start - A_BLOCK, 2 * A_BLOCK)
            kk, vv = k_ref[win, :], v_ref[win, :]
        else:
            prev = rows(dil, start - A_BLOCK * dil)
            kk = jnp.concatenate([k_ref[prev, :], k_ref[cur, :]], axis=0)
            vv = jnp.concatenate([v_ref[prev, :], v_ref[cur, :]], axis=0)
        m, l, acc = _attn_scores(q, kk, vv, bias_ref, d_idx, first)
        if d_idx > 0:
            m0, l0, a0 = m_scr[cur, :], l_scr[cur, :], acc_scr[cur, :]
            mx = jnp.maximum(m0, m)
            w0 = jnp.exp(m0 - mx)
            w1 = jnp.exp(m - mx)
            m, l, acc = mx, w0 * l0 + w1 * l, w0 * a0 + w1 * acc
        m_scr[cur, :] = m
        l_scr[cur, :] = l
        acc_scr[cur, :] = acc

    for d_idx, dil in enumerate(A_DILS):
        n_blocks = seq // (A_BLOCK * dil)

        def residue(r, carry, d_idx=d_idx, dil=dil, n_blocks=n_blocks):
            block(d_idx, dil, r, True)

            def later(n, c):
                block(d_idx, dil, n * (A_BLOCK * dil) + r, False)
                return c

            if n_blocks > 1:
                lax.fori_loop(1, n_blocks, later, 0)
            return carry

        if dil == 1:
            residue(0, 0)
        else:
            lax.fori_loop(0, dil, residue, 0)

    attn_ref[...] = acc_scr[...] / l_scr[...]

    p_idx = pl.program_id(1)
    u = u_ref[...]
    pad_scr[0:POOL_PAD, :] = jnp.zeros((POOL_PAD, LANE), F32)
    s = u
    shift = 1
    for level in range(len(POOL_WINDOWS)):
        pad_scr[POOL_PAD:POOL_PAD + seq, :] = s
        nxt = s + pad_scr[POOL_PAD - shift:POOL_PAD - shift + seq, :]
        s = nxt if level == 0 else jnp.where(p_idx >= level, nxt, s)
        shift *= 2
    pos = lax.broadcasted_iota(jnp.int32, (seq, 1), 0)
    window = lax.shift_left(jnp.int32(2), p_idx)
    cnt = jnp.minimum(pos + 1, window).astype(F32)
    pooled = s / cnt - u
    pool_ref[...] = jnp.dot(pooled.astype(BF16), wp_ref[...], preferred_element_type=F32) * sc_ref[...]


def _even_prompt(q, k, v, u, bias, w_pool, scale, batch, seq):
    m = batch * seq
    col = lambda b, p: (b, p)
    blk = pl.BlockSpec((seq, LANE), col)
    return pl.pallas_call(
        functools.partial(_even_prompt_kernel, seq),
        grid=(batch, N_PAIRS),
        in_specs=[blk, blk, blk, blk,
                  pl.BlockSpec((None,) + bias.shape[1:], lambda b, p: (p, 0, 0, 0, 0)),
                  pl.BlockSpec((None, POOL_GROUP_DIM, POOL_GROUP_DIM), lambda b, p: (p, 0, 0)),
                  pl.BlockSpec((None, 1, POOL_GROUP_DIM), lambda b, p: (p, 0, 0))],
        out_specs=[blk, blk],
        out_shape=[jax.ShapeDtypeStruct((m, A_WIDTH), F32), jax.ShapeDtypeStruct((m, POOL_WIDTH), F32)],
        scratch_shapes=[pltpu.VMEM((seq, LANE), F32), pltpu.VMEM((seq, LANE), F32),
                        pltpu.VMEM((seq, LANE), F32), pltpu.VMEM((POOL_PAD + seq, LANE), F32)],
        compiler_params=_cparams(("parallel", "arbitrary")),
        name="even_prompt",
    )(q, k, v, u, bias, w_pool, scale)


def _even_decode_kernel(q_ref, kn_ref, vn_ref, k1_ref, k4_ref, k16_ref, v1_ref, v4_ref, v16_ref, o_ref):
    q = q_ref[...] * A_SCALE
    kn = kn_ref[...]
    vn = vn_ref[...]
    s_new = jnp.sum(q * kn, axis=-1, keepdims=True)
    head = lax.broadcasted_iota(jnp.int32, (1, A_HEADS, 1), 1).astype(F32)
    slopes = jnp.exp2(-8.0 * (head + 1.0) / A_HEADS)
    dist = (A_KEYS - lax.broadcasted_iota(jnp.int32, (A_KEYS, 1, 1), 0)).astype(F32)
    outs, lses = [], []
    for dil, k_ref, v_ref in ((1, k1_ref, v1_ref), (4, k4_ref, v4_ref), (16, k16_ref, v16_ref)):
        kh = k_ref[...]
        vh = v_ref[...]
        s = jnp.sum(kh * q[None], axis=-1, keepdims=True) - slopes * (float(dil) * dist)
        m = jnp.maximum(jnp.max(s, axis=0), s_new)
        p = jnp.exp(s - m[None])
        p_new = jnp.exp(s_new - m)
        l = jnp.sum(p, axis=0) + p_new
        o = (jnp.sum(p * vh, axis=0) + p_new * vn) / l
        outs.append(o)
        lses.append(m + jnp.log(l))
    top = jnp.maximum(jnp.maximum(lses[0], lses[1]), lses[2])
    ws = [jnp.exp(x - top) for x in lses]
    tot = ws[0] + ws[1] + ws[2]
    o_ref[...] = (ws[0] * outs[0] + ws[1] * outs[1] + ws[2] * outs[2]) / tot


def _even_decode(layer, q3, kn3, vn3, state_k, state_v):
    n_layers, nb, hist = state_k.shape[:3]
    tail = (A_HEADS, HEAD_DIM)
    views, specs = [], []
    for state in (state_k, state_v):
        for dil in A_DILS:
            if dil == 1:
                views.append(state)
                specs.append(pl.BlockSpec((None, None, A_KEYS) + tail,
                                          lambda b: (layer, b, hist // A_KEYS - 1, 0, 0)))
            else:
                views.append(state.reshape(n_layers, nb, hist // dil, dil, *tail))
                specs.append(pl.BlockSpec((None, None, A_KEYS, None) + tail,
                                          lambda b, dil=dil: (layer, b, hist // (dil * A_KEYS) - 1, 0, 0, 0)))
    tok = pl.BlockSpec((None,) + tail, lambda b: (b, 0, 0))
    return pl.pallas_call(
        _even_decode_kernel,
        grid=(nb,),
        in_specs=[tok, tok, tok] + specs,
        out_specs=tok,
        out_shape=jax.ShapeDtypeStruct((nb,) + tail, F32),
        compiler_params=_cparams(("parallel",)),
        name="even_decode",
    )(q3, kn3, vn3, *views)


def _pool_decode_kernel(u_ref, hist_ref, wp_ref, sc_ref, o_ref):
    u = u_ref[...]
    for g, w in enumerate(POOL_WINDOWS):
        cols = slice(g * POOL_GROUP_DIM, (g + 1) * POOL_GROUP_DIM)
        cur = u[:, cols]
        tot = cur
        for j in range(POOL_HIST - (w - 1), POOL_HIST):
            tot = tot + hist_ref[j, :, cols]
        pooled = tot / float(w) - cur
        o_ref[:, cols] = jnp.dot(pooled.astype(BF16), wp_ref[g], preferred_element_type=F32) * sc_ref[:, cols]


def _pool_decode(u, hist_t, w_pool, scale):
    nb = u.shape[0]
    return pl.pallas_call(
        _pool_decode_kernel,
        out_shape=jax.ShapeDtypeStruct((nb, POOL_WIDTH), F32),
        compiler_params=pltpu.CompilerParams(vmem_limit_bytes=VMEM_LIMIT),
        name="pool_decode",
    )(u, hist_t, w_pool, scale)


KV_SHIFT_LAG = 2


def _kv_shift_kernel(sk_ref, sv_ref, nk_ref, nv_ref, ok_ref, ov_ref, sem):
    n_layers, nb, hist = sk_ref.shape[:3]
    streams = [(i, j, s_ref, n_ref, o_ref) for i in range(n_layers)
               for j, (s_ref, n_ref, o_ref) in enumerate(((sk_ref, nk_ref, ok_ref), (sv_ref, nv_ref, ov_ref)))]

    def history_copy(i, j, s_ref, o_ref, b):
        return pltpu.make_async_copy(s_ref.at[i, b, pl.ds(1, hist - 1)], o_ref.at[i, b, pl.ds(0, hist - 1)],
                                     sem.at[0, i, j])

    new_rows = [pltpu.make_async_copy(n_ref.at[i], o_ref.at[i, :, pl.ds(hist - 1, 1)], sem.at[1, i, j])
                for i, j, _, n_ref, o_ref in streams]
    for c in new_rows:
        c.start()

    def step(b, carry):
        for i, j, s_ref, _, o_ref in streams:
            history_copy(i, j, s_ref, o_ref, b).start()

        @pl.when(b >= KV_SHIFT_LAG)
        def _():
            for i, j, s_ref, _, o_ref in streams:
                history_copy(i, j, s_ref, o_ref, b - KV_SHIFT_LAG).wait()

        return carry

    lax.fori_loop(0, nb, step, 0)
    for b in range(nb - KV_SHIFT_LAG, nb):
        for i, j, s_ref, _, o_ref in streams:
            history_copy(i, j, s_ref, o_ref, b).wait()
    for c in new_rows:
        c.wait()


def _kv_shift(state_k, state_v, new_k, new_v):
    n_layers = state_k.shape[0]
    any_spec = pl.BlockSpec(memory_space=pl.ANY)
    return pl.pallas_call(
        _kv_shift_kernel,
        in_specs=[any_spec] * 4,
        out_specs=[any_spec] * 2,
        out_shape=[jax.ShapeDtypeStruct(state_k.shape, F32), jax.ShapeDtypeStruct(state_v.shape, F32)],
        scratch_shapes=[pltpu.SemaphoreType.DMA((2, n_layers, 2))],
        name="kv_shift",
    )(state_k, state_v, new_k, new_v)


def _gates(ba, alog_ref, dtb_ref):
    beta = jax.nn.sigmoid(ba)
    g = -jnp.exp(alog_ref[...]) * _softplus(ba + dtb_ref[...])
    return beta, g


def _l2norm(x):
    return x * lax.rsqrt(jnp.sum(x * x, axis=-1, keepdims=True) + 1e-6)


def _gated_out(o, gate, nw):
    o = o * lax.rsqrt(jnp.mean(o * o, axis=-1, keepdims=True) + RMS_EPS) * nw
    return o * _silu(gate)


def _gdn_prompt_kernel(qkv_ref, halo_ref, gate_ref, ba_ref, wc_ref, alog_ref, dtb_ref, nw_ref,
                       o_ref, sfin_ref, s_scr, xx_scr):
    step = pl.program_id(1)
    rows = GDN_ROWS

    @pl.when(step == 0)
    def _():
        s_scr[...] = jnp.zeros(s_scr.shape, F32)

    xx_scr[0:HALO, :] = jnp.where(step == 0, 0.0, halo_ref[...])
    xx_scr[HALO:HALO + rows, :] = qkv_ref[...]

    def conv(c0):
        cols = slice(c0, c0 + GDN_DK)
        acc = wc_ref[CONV_WIDTH - 1:CONV_WIDTH, cols] * xx_scr[HALO:HALO + rows, cols]
        for i in range(1, CONV_WIDTH):
            acc = acc + wc_ref[CONV_WIDTH - 1 - i:CONV_WIDTH - i, cols] * xx_scr[HALO - i:HALO - i + rows, cols]
        return _silu(acc)

    beta_all, g_all = _gates(ba_ref[...], alog_ref, dtb_ref)
    row = lax.broadcasted_iota(jnp.int32, (rows, LANE), 0)
    in_chunk = jnp.bitwise_and(row, GDN_CHUNK - 1)
    gcs = g_all
    shift = 1
    while shift < GDN_CHUNK:
        gcs = gcs + jnp.where(in_chunk >= shift, pltpu.roll(gcs, shift, axis=0), 0.0)
        shift *= 2
    glast = jnp.concatenate(
        [jnp.broadcast_to(gcs[(c + 1) * GDN_CHUNK - 1:(c + 1) * GDN_CHUNK, :], (GDN_CHUNK, LANE))
         for c in range(GDN_NC)], axis=0)
    eg_all = jnp.exp(gcs)
    ekd_all = jnp.exp(glast - gcs)
    gcs_t = gcs.T
    eglast_t = jnp.exp(gcs_t)

    ri = lax.broadcasted_iota(jnp.int32, (rows, rows), 0)
    ci = lax.broadcasted_iota(jnp.int32, (rows, rows), 1)
    same = lax.shift_right_logical(ri, 6) == lax.shift_right_logical(ci, 6)
    bd_tril = jnp.logical_and(same, ri >= ci)
    bd_strict = jnp.logical_and(same, ri > ci)
    eye_cat = (lax.broadcasted_iota(jnp.int32, (GDN_CHUNK, rows), 0)
               == jnp.bitwise_and(lax.broadcasted_iota(jnp.int32, (GDN_CHUNK, rows), 1), GDN_CHUNK - 1)
               ).astype(F32)

    def to_cat(x):
        out = x[0:GDN_CHUNK]
        for c in range(1, GDN_NC):
            out = out + x[c * GDN_CHUNK:(c + 1) * GDN_CHUNK]
        return out

    def to_bd(x):
        return jnp.where(same, jnp.concatenate([x] * GDN_NC, axis=0), 0.0)

    for h in range(GDN_HEADS):
        gi = GDN_HEADS + h
        beta = beta_all[:, h:h + 1]
        gcol = gcs[:, gi:gi + 1]
        grow = gcs_t[gi:gi + 1, :]
        egc = eg_all[:, gi:gi + 1]
        ekd = ekd_all[:, gi:gi + 1]
        q = _l2norm(conv(h * GDN_DK)) * (GDN_DK ** -0.5)
        k = _l2norm(conv(GDN_QK + h * GDN_DK))
        v = conv(2 * GDN_QK + h * GDN_DK)
        kb = k * beta
        vb = v * beta
        prod = _dot_nt(jnp.concatenate([kb, q], axis=0), k)
        diff = gcol - grow
        decay = jnp.where(bd_tril, jnp.exp(jnp.where(bd_tril, diff, 0.0)), 0.0)
        lmat = jnp.where(bd_strict, prod[:rows] * decay, 0.0)
        a_intra = prod[rows:] * decay
        p_cat = -to_cat(lmat)
        t_cat = eye_cat + p_cat
        p_cat = _dot3(p_cat, -lmat)
        n_levels = 6
        for level in range(1, n_levels):
            p_bd = to_bd(p_cat)
            if level < n_levels - 1:
                res = _dot3(jnp.concatenate([p_cat, t_cat], axis=0), p_bd)
                p_cat, t_cat = res[:GDN_CHUNK], t_cat + res[GDN_CHUNK:]
            else:
                t_cat = t_cat + _dot3(t_cat, p_bd)
        tinv = to_bd(t_cat)
        uw = _dot(tinv, jnp.concatenate([vb, kb * egc], axis=1))
        u, w = uw[:, :GDN_DK], uw[:, GDN_DK:]
        qdec = q * egc
        kdec = k * ekd
        s = s_scr[h]
        v_news, q_states = [], []
        for c in range(GDN_NC):
            r = slice(c * GDN_CHUNK, (c + 1) * GDN_CHUNK)
            res = _dot(jnp.concatenate([w[r], qdec[r]], axis=0), s)
            v_new = u[r] - res[:GDN_CHUNK]
            q_states.append(res[GDN_CHUNK:])
            v_news.append(v_new)
            last = (c + 1) * GDN_CHUNK - 1
            s = s * eglast_t[gi:gi + 1, last:last + 1] + _dot_tn(kdec[r], v_new)
        s_scr[h] = s
        o = jnp.concatenate(q_states, axis=0) + _dot(a_intra, jnp.concatenate(v_news, axis=0))
        cols = slice(h * GDN_DK, (h + 1) * GDN_DK)
        o_ref[:, cols] = _gated_out(o, gate_ref[:, cols], nw_ref[...])

    @pl.when(step == pl.num_programs(1) - 1)
    def _():
        sfin_ref[...] = s_scr[...]


def _gdn_prompt(qkv, gate, ba, w_conv, alog, dtb, nw, batch, seq):
    m = batch * seq
    steps = seq // GDN_ROWS
    per_halo = GDN_ROWS // HALO
    row_blk = lambda b, c: (b * steps + c, 0)
    return pl.pallas_call(
        _gdn_prompt_kernel,
        grid=(batch, steps),
        in_specs=[pl.BlockSpec((GDN_ROWS, GDN_CONV_CH), row_blk),
                  pl.BlockSpec((HALO, GDN_CONV_CH),
                               lambda b, c: (jnp.maximum((b * steps + c) * per_halo - 1, 0), 0)),
                  pl.BlockSpec((GDN_ROWS, GDN_VW), row_blk),
                  pl.BlockSpec((GDN_ROWS, LANE), row_blk),
                  _resident(w_conv.shape), _resident(alog.shape), _resident(dtb.shape), _resident(nw.shape)],
        out_specs=[pl.BlockSpec((GDN_ROWS, GDN_VW), row_blk),
                   pl.BlockSpec((None, GDN_HEADS, GDN_DK, GDN_DK), lambda b, c: (b, 0, 0, 0))],
        out_shape=[jax.ShapeDtypeStruct((m, GDN_VW), F32),
                   jax.ShapeDtypeStruct((batch, GDN_HEADS, GDN_DK, GDN_DK), F32)],
        scratch_shapes=[pltpu.VMEM((GDN_HEADS, GDN_DK, GDN_DK), F32),
                        pltpu.VMEM((HALO + GDN_ROWS, GDN_CONV_CH), F32)],
        compiler_params=_cparams(("parallel", "arbitrary")),
        name="gdn_prompt",
    )(qkv, qkv, gate, ba, w_conv, alog, dtb, nw)


def _gdn_decode_prep_kernel(x_ref, hist_ref, ba_ref, wc_ref, alog_ref, dtb_ref, q_ref, k_ref, v_ref, bg_ref):
    for h in range(3 * GDN_HEADS):
        cols = slice(h * GDN_DK, (h + 1) * GDN_DK)
        acc = wc_ref[CONV_WIDTH - 1:CONV_WIDTH, cols] * x_ref[:, cols]
        for i in range(1, CONV_WIDTH):
            acc = acc + wc_ref[CONV_WIDTH - 1 - i:CONV_WIDTH - i, cols] * hist_ref[CONV_WIDTH - 1 - i, :, cols]
        y = _silu(acc)
        if h < GDN_HEADS:
            q_ref[:, cols] = _l2norm(y) * (GDN_DK ** -0.5)
        elif h < 2 * GDN_HEADS:
            k_ref[:, slice((h - GDN_HEADS) * GDN_DK, (h - GDN_HEADS + 1) * GDN_DK)] = _l2norm(y)
        else:
            v_ref[:, slice((h - 2 * GDN_HEADS) * GDN_DK, (h - 2 * GDN_HEADS + 1) * GDN_DK)] = y
    beta, g = _gates(ba_ref[...], alog_ref, dtb_ref)
    lane = lax.broadcasted_iota(jnp.int32, beta.shape, 1)
    bg_ref[...] = jnp.where(lane < GDN_HEADS, beta, g)


def _gdn_decode_prep(x, hist_t, ba, w_conv, alog, dtb):
    nb = x.shape[0]
    head = jax.ShapeDtypeStruct((nb, GDN_QK), F32)
    return pl.pallas_call(
        _gdn_decode_prep_kernel,
        out_shape=[head, head, head, jax.ShapeDtypeStruct((nb, LANE), F32)],
        compiler_params=pltpu.CompilerParams(vmem_limit_bytes=VMEM_LIMIT),
        name="gdn_decode_prep",
    )(x, hist_t, ba, w_conv, alog, dtb)


def _gdn_decode_kernel(q_ref, k_ref, v_ref, gate_ref, beta_ref, g_ref, s_ref, nw_ref, o_ref, so_ref):
    q_all, k_all, v_all = q_ref[...], k_ref[...], v_ref[...]
    gate_all = gate_ref[...]
    beta_all = beta_ref[...]
    eg_all = jnp.exp(g_ref[...])
    qk_all = jnp.sum(q_all * k_all, axis=-1, keepdims=True)
    pad = jnp.zeros((6, GDN_DK), F32)
    outs = []
    for h in range(GDN_HEADS):
        s = s_ref[h]
        q, k, v = q_all[h:h + 1], k_all[h:h + 1], v_all[h:h + 1]
        beta, eg = beta_all[h:h + 1], eg_all[h:h + 1]
        res = _dot(jnp.concatenate([k, q, pad], axis=0), s)
        v_new = beta * (v - eg * res[0:1])
        o = eg * res[1:2] + qk_all[h:h + 1] * v_new
        k_col = jnp.broadcast_to(k, (GDN_DK, GDN_DK)).T
        so_ref[h] = s * eg + k_col * v_new
        outs.append(o)
    o_all = jnp.concatenate(outs, axis=0)
    o_ref[...] = _gated_out(o_all, gate_all, nw_ref[...])


def _gdn_decode(layer, q3, k3, v3, gate3, beta3, g3, state, nw):
    nb = q3.shape[0]
    tok = pl.BlockSpec((None, GDN_HEADS, GDN_DK), lambda b: (b, 0, 0))
    sc = pl.BlockSpec((None, GDN_HEADS, 1), lambda b: (b, 0, 0))
    return pl.pallas_call(
        _gdn_decode_kernel,
        grid=(nb,),
        in_specs=[tok, tok, tok, tok, sc, sc,
                  pl.BlockSpec((None, None, GDN_HEADS, GDN_DK, GDN_DK), lambda b: (layer, b, 0, 0, 0)),
                  _resident(nw.shape)],
        out_specs=[tok, pl.BlockSpec((None, GDN_HEADS, GDN_DK, GDN_DK), lambda b: (b, 0, 0, 0))],
        out_shape=[jax.ShapeDtypeStruct((nb, GDN_HEADS, GDN_DK), F32),
                   jax.ShapeDtypeStruct((nb, GDN_HEADS, GDN_DK, GDN_DK), F32)],
        compiler_params=_cparams(("parallel",)),
        name="gdn_decode",
    )(q3, k3, v3, gate3, beta3, g3, state, nw)


def _pad_cols(x, start, width=LANE):
    return jnp.pad(x, ((0, 0), (start, width - start - x.shape[1])))


def kernel(x_prompt, x_sample, state_a_k, state_a_v, state_pool, state_conv, state_gdn, norm_mix, norm_ffn,
           w_in_a, w_out_a, w_pool, pool_scale, w_in_c, w_conv_c, a_log_c, dt_bias_c, norm_out_c, w_out_c,
           w_ffn_gate, w_ffn_up, w_ffn_down, norm_final):
    batch, seq, d = x_prompt.shape
    nb = x_sample.shape[0]
    depth = norm_mix.shape[0]
    tm = 512

    w_in_a_b = w_in_a.astype(BF16)
    ws_even = [[w_in_a_b[i][:, j * A_WIDTH:(j + 1) * A_WIDTH] for j in range(4)] for i in range(w_in_a.shape[0])]
    w_out_a_b = w_out_a.astype(BF16)
    w_pool_b = w_pool.astype(BF16)
    w_in_c_b = w_in_c.astype(BF16)
    ws_odd = [[w_in_c_b[i][:, :GDN_CONV_CH], w_in_c_b[i][:, GDN_CONV_CH:GDN_CONV_CH + GDN_VW],
               _pad_cols(w_in_c_b[i][:, GDN_CONV_CH + GDN_VW:], 0)] for i in range(w_in_c.shape[0])]
    w_out_c_b = w_out_c.astype(BF16)
    wg_b, wu_b, wd_b = w_ffn_gate.astype(BF16), w_ffn_up.astype(BF16), w_ffn_down.astype(BF16)
    alog_p = _pad_cols(a_log_c, GDN_HEADS)
    dtb_p = _pad_cols(dt_bias_c, GDN_HEADS)
    bias = _attn_bias()
    g_final = norm_final.reshape(1, d)

    def ffn_args(layer):
        last = layer == depth - 1
        return (norm_ffn[layer:layer + 1], wg_b[layer], wu_b[layer], wd_b[layer], g_final if last else None)

    xs = x_sample.reshape(nb, d)
    new_k, new_v, pool_s, conv_s, gdn_s = [], [], [], [], []
    for layer in range(depth):
        i = layer // 2
        g_mix = norm_mix[layer:layer + 1]
        if layer % 2 == 0:
            q, k, v, u = _norm_proj(xs, g_mix, ws_even[i], nb)
            shape3 = (nb, A_HEADS, HEAD_DIM)
            attn = _even_decode(i, q.reshape(shape3), k.reshape(shape3), v.reshape(shape3), state_a_k, state_a_v)
            pooled = _pool_decode(u, state_pool[i].transpose(1, 0, 2), w_pool_b[i], pool_scale[i:i + 1])
            new_k.append(k.reshape(nb, 1, A_HEADS, HEAD_DIM))
            new_v.append(v.reshape(nb, 1, A_HEADS, HEAD_DIM))
            pool_s.append(jnp.concatenate([state_pool[i][:, 1:], u[:, None, :]], axis=1))
            mixes = [attn.reshape(nb, A_WIDTH), pooled]
            wos = [w_out_a_b[i][:A_WIDTH], w_out_a_b[i][A_WIDTH:]]
        else:
            qkv, gate, ba = _norm_proj(xs, g_mix, ws_odd[i], nb)
            qn, kn, vv, bg = _gdn_decode_prep(qkv, state_conv[i].transpose(1, 0, 2), ba, w_conv_c[i],
                                              alog_p[i:i + 1], dtb_p[i:i + 1])
            shape3 = (nb, GDN_HEADS, GDN_DK)
            o3, s_new = _gdn_decode(i, qn.reshape(shape3), kn.reshape(shape3), vv.reshape(shape3),
                                    gate.reshape(shape3), bg[:, :GDN_HEADS].reshape(nb, GDN_HEADS, 1),
                                    bg[:, GDN_HEADS:2 * GDN_HEADS].reshape(nb, GDN_HEADS, 1),
                                    state_gdn, norm_out_c[i:i + 1])
            conv_s.append(jnp.concatenate([state_conv[i][:, 1:], qkv[:, None, :]], axis=1))
            gdn_s.append(s_new)
            mixes = [o3.reshape(nb, GDN_VW)]
            wos = [w_out_c_b[i]]
        xs = _mix_ffn(xs, mixes, wos, *ffn_args(layer), nb)
    y_sample = xs.reshape(nb, 1, d)
    a_k_sample, a_v_sample = _kv_shift(state_a_k, state_a_v, jnp.stack(new_k), jnp.stack(new_v))

    xp = x_prompt.reshape(batch * seq, d)
    ak_p, av_p, pool_p, conv_p, gdn_p = [], [], [], [], []
    for layer in range(depth):
        i = layer // 2
        g_mix = norm_mix[layer:layer + 1]
        if layer % 2 == 0:
            q, k, v, u = _norm_proj(xp, g_mix, ws_even[i], tm)
            attn, pooled = _even_prompt(q, k, v, u, bias, w_pool_b[i],
                                        pool_scale[i].reshape(len(POOL_WINDOWS), 1, POOL_GROUP_DIM), batch, seq)
            ak_p.append(k.reshape(batch, seq, A_HEADS, HEAD_DIM))
            av_p.append(v.reshape(batch, seq, A_HEADS, HEAD_DIM))
            pool_p.append(u.reshape(batch, seq, POOL_WIDTH)[:, seq - POOL_HIST:])
            mixes = [attn, pooled]
            wos = [w_out_a_b[i][:A_WIDTH], w_out_a_b[i][A_WIDTH:]]
        else:
            qkv, gate, ba = _norm_proj(xp, g_mix, ws_odd[i], tm)
            o, s_fin = _gdn_prompt(qkv, gate, ba, w_conv_c[i], alog_p[i:i + 1], dtb_p[i:i + 1],
                                   norm_out_c[i:i + 1], batch, seq)
            conv_p.append(qkv.reshape(batch, seq, GDN_CONV_CH)[:, seq - (CONV_WIDTH - 1):])
            gdn_p.append(s_fin)
            mixes = [o]
            wos = [w_out_c_b[i]]
        xp = _mix_ffn(xp, mixes, wos, *ffn_args(layer), tm)
    y_prompt = xp.reshape(batch, seq, d)

    return (y_prompt, y_sample, jnp.stack(ak_p), jnp.stack(av_p), jnp.stack(pool_p), jnp.stack(conv_p),
            jnp.stack(gdn_p), a_k_sample, a_v_sample, jnp.stack(pool_s), jnp.stack(conv_s), jnp.stack(gdn_s))
```

```python
import functools

import jax
import jax.numpy as jnp
from jax import lax
from jax.experimental import pallas as pl
from jax.experimental.pallas import tpu as pltpu

F32 = jnp.float32
BF16 = jnp.bfloat16

D_MODEL = 1024
RMS_EPS = 1e-6
NEG = -1e30

HEAD_DIM = 64
A_HEADS = 8
A_WIDTH = A_HEADS * HEAD_DIM
A_BRANCHES = ((128, 1), (512, 4), (2048, 16))
A_DILS = tuple(d for _, d in A_BRANCHES)
A_KEYS = 128
A_BLOCK = 128
A_SCALE = HEAD_DIM ** -0.5
N_PAIRS = A_HEADS // 2
A_INFLIGHT = 4

POOL_WINDOWS = (2, 4, 8, 16)
POOL_WIDTH = 512
POOL_GROUP_DIM = 128
POOL_HIST = 15
POOL_PAD = 16

GDN_HEADS = 8
GDN_DK = 128
GDN_QK = GDN_HEADS * GDN_DK
GDN_VW = GDN_HEADS * GDN_DK
GDN_CONV_CH = 3 * GDN_QK
CONV_WIDTH = 4
GDN_CHUNK = 64
GDN_CHUNK_SHIFT = GDN_CHUNK.bit_length() - 1
GDN_ROWS = 256
GDN_NC = GDN_ROWS // GDN_CHUNK
GDN_GROUP = 4
HALO = 8

FFN_HIDDEN = 2816
FFN_CHUNK = 256
LANE = 128

VMEM_LIMIT = 56 * 1024 * 1024


def _cparams(sem):
    return pltpu.CompilerParams(dimension_semantics=sem, vmem_limit_bytes=VMEM_LIMIT)


def _resident(shape):
    nd = len(shape)
    return pl.BlockSpec(shape, lambda *_: (0,) * nd, pipeline_mode=pl.Buffered(1))


def _rms(x, g):
    ms = jnp.mean(x * x, axis=-1, keepdims=True)
    return x * lax.rsqrt(ms + RMS_EPS) * g


def _silu(x):
    return x * jax.nn.sigmoid(x)


def _softplus(x):
    return jnp.maximum(x, 0.0) + jnp.log1p(jnp.exp(-jnp.abs(x)))


def _dot(a, b):
    return jnp.dot(a.astype(BF16), b.astype(BF16), preferred_element_type=F32)


def _dot_nt(a, b):
    return lax.dot_general(a.astype(BF16), b.astype(BF16), (((1,), (1,)), ((), ())),
                           preferred_element_type=F32)


def _dot_tn(a, b):
    return lax.dot_general(a.astype(BF16), b.astype(BF16), (((0,), (0,)), ((), ())),
                           preferred_element_type=F32)


def _split(x):
    hi = x.astype(BF16)
    lo = (x - hi.astype(F32)).astype(BF16)
    return hi, lo


def _dot3(a, b):
    ah, al = _split(a)
    bh, bl = _split(b)
    out = jnp.dot(ah, bh, preferred_element_type=F32)
    out = out + jnp.dot(al, bh, preferred_element_type=F32)
    return out + jnp.dot(ah, bl, preferred_element_type=F32)


def _norm_proj_kernel(n_out, x_ref, g_ref, *refs):
    w_refs, o_refs = refs[:n_out], refs[n_out:]
    h = _rms(x_ref[...], g_ref[...]).astype(BF16)
    for w_ref, o_ref in zip(w_refs, o_refs):
        n = w_ref.shape[1]
        for c0 in range(0, n, 512):
            c1 = min(c0 + 512, n)
            o_ref[:, c0:c1] = jnp.dot(h, w_ref[:, c0:c1], preferred_element_type=F32)


def _norm_proj(x, g, ws, tm):
    m, d = x.shape
    n_out = len(ws)
    return pl.pallas_call(
        functools.partial(_norm_proj_kernel, n_out),
        grid=(m // tm,),
        in_specs=[pl.BlockSpec((tm, d), lambda i: (i, 0)), _resident((1, d))]
        + [_resident(w.shape) for w in ws],
        out_specs=[pl.BlockSpec((tm, w.shape[1]), lambda i: (i, 0)) for w in ws],
        out_shape=[jax.ShapeDtypeStruct((m, w.shape[1]), F32) for w in ws],
        compiler_params=_cparams(("parallel",)),
        name="norm_proj",
    )(x, g, *ws)


def _even_in_proj_kernel(x_ref, g_ref, wq_ref, wk_ref, wv_ref, wu_ref,
                         q_ref, k_ref, v_ref, u_ref, kt_ref, vt_ref):
    h = _rms(x_ref[...], g_ref[...]).astype(BF16)
    q_ref[...] = jnp.dot(h, wq_ref[...], preferred_element_type=F32)
    u_ref[...] = jnp.dot(h, wu_ref[...], preferred_element_type=F32)
    k = jnp.dot(h, wk_ref[...], preferred_element_type=F32)
    k_ref[...] = k
    kt_ref[...] = k.T
    v = jnp.dot(h, wv_ref[...], preferred_element_type=F32)
    v_ref[...] = v
    vt_ref[...] = v.T


def _even_in_proj(x, g, ws, batch, seq, tm):
    m, d = x.shape
    per_seq = seq // tm
    row = pl.BlockSpec((tm, A_WIDTH), lambda i: (i, 0))
    tr = pl.BlockSpec((None, A_WIDTH, tm), lambda i: (i // per_seq, 0, i % per_seq))
    flat = jax.ShapeDtypeStruct((m, A_WIDTH), F32)
    trs = jax.ShapeDtypeStruct((batch, A_WIDTH, seq), F32)
    return pl.pallas_call(
        _even_in_proj_kernel,
        grid=(m // tm,),
        in_specs=[pl.BlockSpec((tm, d), lambda i: (i, 0)), _resident((1, d))] + [_resident(w.shape) for w in ws],
        out_specs=[row, row, row, row, tr, tr],
        out_shape=[flat, flat, flat, flat, trs, trs],
        compiler_params=_cparams(("parallel",)),
        name="even_in_proj",
    )(x, g, *ws)


def _mix_ffn_kernel(n_mix, final, x_ref, *refs):
    a_refs = refs[:n_mix]
    wo_refs = refs[n_mix:2 * n_mix]
    g_ref, wg_ref, wu_ref, wd_ref = refs[2 * n_mix:2 * n_mix + 4]
    rest = refs[2 * n_mix + 4:]
    if final:
        gf_ref, o_ref, act_ref = rest
    else:
        o_ref, act_ref = rest
    x1 = x_ref[...]
    for a_ref, wo_ref in zip(a_refs, wo_refs):
        x1 = x1 + jnp.dot(a_ref[...].astype(BF16), wo_ref[...], preferred_element_type=F32)
    h = _rms(x1, g_ref[...]).astype(BF16)
    for c0 in range(0, FFN_HIDDEN, FFN_CHUNK):
        gate = jnp.dot(h, wg_ref[:, c0:c0 + FFN_CHUNK], preferred_element_type=F32)
        up = jnp.dot(h, wu_ref[:, c0:c0 + FFN_CHUNK], preferred_element_type=F32)
        act_ref[:, c0:c0 + FFN_CHUNK] = (_silu(gate) * up).astype(BF16)
    out = x1 + jnp.dot(act_ref[...], wd_ref[...], preferred_element_type=F32)
    if final:
        out = _rms(out, gf_ref[...])
    o_ref[...] = out


def _mix_ffn(x, mixes, wos, g, wg, wu, wd, g_final, tm):
    m, d = x.shape
    n_mix = len(mixes)
    final = g_final is not None
    in_specs = [pl.BlockSpec((tm, d), lambda i: (i, 0))]
    in_specs += [pl.BlockSpec((tm, a.shape[1]), lambda i: (i, 0)) for a in mixes]
    in_specs += [_resident(w.shape) for w in wos]
    in_specs += [_resident(g.shape), _resident(wg.shape), _resident(wu.shape), _resident(wd.shape)]
    args = [x, *mixes, *wos, g, wg, wu, wd]
    if final:
        in_specs.append(_resident(g_final.shape))
        args.append(g_final)
    return pl.pallas_call(
        functools.partial(_mix_ffn_kernel, n_mix, final),
        grid=(m // tm,),
        in_specs=in_specs,
        out_specs=pl.BlockSpec((tm, d), lambda i: (i, 0)),
        out_shape=jax.ShapeDtypeStruct((m, d), F32),
        scratch_shapes=[pltpu.VMEM((tm, FFN_HIDDEN), BF16)],
        compiler_params=_cparams(("parallel",)),
        name="mix_ffn",
    )(*args)


def _alibi_slope(h):
    return 2.0 ** (-8.0 * (h + 1.0) / A_HEADS)


def _attn_bias():
    rel = jnp.arange(A_BLOCK)[:, None] + A_BLOCK - jnp.arange(2 * A_BLOCK)[None, :]
    band = (rel >= 0) & (rel <= A_KEYS)
    slopes = jnp.exp2(-8.0 * (jnp.arange(A_HEADS, dtype=F32) + 1.0) / A_HEADS)
    dils = jnp.asarray(A_DILS, F32)
    bias = -slopes[None, :, None, None] * (rel.astype(F32)[None, None] * dils[:, None, None, None])
    bias = jnp.where(band[None, None], bias, NEG)
    bias = bias.reshape(len(A_DILS), N_PAIRS, 2, A_BLOCK, 2 * A_BLOCK)
    return bias.transpose(1, 0, 2, 3, 4)


def _attn_scores(q, kk, vv, bias_ref, d_idx, first):
    lane = lax.broadcasted_iota(jnp.int32, (A_BLOCK, LANE), 1)
    lo = lane < HEAD_DIM
    kb = kk.astype(BF16)
    vb = vv.astype(BF16)
    stats = []
    for h in range(2):
        qm = jnp.where(lo if h == 0 else jnp.logical_not(lo), q * A_SCALE, 0.0)
        s = _dot_nt(qm, kb)
        if first:
            s = s + bias_ref[d_idx, h, :, A_BLOCK:]
        else:
            s = s + bias_ref[d_idx, h]
        m = jnp.max(s, axis=-1, keepdims=True)
        p = jnp.exp(s - m)
        l = jnp.sum(p, axis=-1, keepdims=True)
        pv = jnp.dot(p.astype(BF16), vb, preferred_element_type=F32)
        stats.append((m, l, pv))
    m = jnp.where(lo, stats[0][0], stats[1][0])
    l = jnp.where(lo, stats[0][1], stats[1][1])
    acc = jnp.where(lo, stats[0][2], stats[1][2])
    return m, l, acc


def _even_prompt_kernel(seq, q_ref, k_ref, v_ref, u_ref, bias_ref, wp_ref, sc_ref,
                        attn_ref, pool_ref, m_scr, l_scr, acc_scr, pad_scr):
    def rows(dil, start):
        if dil == 1:
            return pl.ds(start, A_BLOCK)
        return pl.ds(start, A_BLOCK, stride=dil)

    def compute(d_idx, dil, start, first):
        cur = rows(dil, start)
        q = q_ref[cur, :]
        if first:
            kk, vv = k_ref[cur, :], v_ref[cur, :]
        elif dil == 1:
            win = pl.ds(start - A_BLOCK, 2 * A_BLOCK)
            kk, vv = k_ref[win, :], v_ref[win, :]
        else:
            prev = rows(dil, start - A_BLOCK * dil)
            kk = jnp.concatenate([k_ref[prev, :], k_ref[cur, :]], axis=0)
            vv = jnp.concatenate([v_ref[prev, :], v_ref[cur, :]], axis=0)
        m, l, acc = _attn_scores(q, kk, vv, bias_ref, d_idx, first)
        if d_idx != first_branch:
            m0, l0, a0 = m_scr[cur, :], l_scr[cur, :], acc_scr[cur, :]
            mx = jnp.maximum(m0, m)
            w0 = jnp.exp(m0 - mx)
            w1 = jnp.exp(m - mx)
            m, l, acc = mx, w0 * l0 + w1 * l, w0 * a0 + w1 * acc
        return cur, m, l, acc

    def blocks(d_idx, dil, todo):
        done = [compute(d_idx, dil, start, first) for start, first in todo]
        for cur, m, l, acc in done:
            m_scr[cur, :] = m
            l_scr[cur, :] = l
            acc_scr[cur, :] = acc

    first_branch = len(A_DILS) - 1
    for d_idx, dil in reversed(list(enumerate(A_DILS))):
        span = A_BLOCK * dil
        n_blocks = seq // span
        if dil == 1:
            blocks(d_idx, dil, [(0, True)])
            group = 3

            def later(g, c, d_idx=d_idx, dil=dil, span=span, group=group):
                blocks(d_idx, dil, [((1 + g * group + j) * span, False) for j in range(group)])
                return c

            lax.fori_loop(0, (n_blocks - 1) // group, later, 0)
        elif n_blocks > 1:
            def residue(r, c, d_idx=d_idx, dil=dil, span=span, n_blocks=n_blocks):
                blocks(d_idx, dil, [(r, True)] + [(n * span + r, False) for n in range(1, n_blocks)])
                return c

            lax.fori_loop(0, dil, residue, 0)
        else:
            def residues(g, c, d_idx=d_idx, dil=dil):
                blocks(d_idx, dil, [(g * A_INFLIGHT + j, True) for j in range(A_INFLIGHT)])
                return c

            lax.fori_loop(0, dil // A_INFLIGHT, residues, 0)

    attn_ref[...] = acc_scr[...] / l_scr[...]

    p_idx = pl.program_id(1)
    u = u_ref[...]
    pad_scr[0:POOL_PAD, :] = jnp.zeros((POOL_PAD, LANE), F32)
    s = u
    shift = 1
    for level in range(len(POOL_WINDOWS)):
        pad_scr[POOL_PAD:POOL_PAD + seq, :] = s
        nxt = s + pad_scr[POOL_PAD - shift:POOL_PAD - shift + seq, :]
        s = nxt if level == 0 else jnp.where(p_idx >= level, nxt, s)
        shift *= 2
    pos = lax.broadcasted_iota(jnp.int32, (seq, 1), 0)
    window = lax.shift_left(jnp.int32(2), p_idx)
    cnt = jnp.minimum(pos + 1, window).astype(F32)
    pooled = s / cnt - u
    pool_ref[...] = jnp.dot(pooled.astype(BF16), wp_ref[...], preferred_element_type=F32) * sc_ref[...]


def _even_prompt(q, k, v, u, bias, w_pool, scale, batch, seq):
    m = batch * seq
    assert (seq // A_BLOCK - 1) % 3 == 0 and seq % (A_BLOCK * max(A_DILS)) == 0
    assert all(dil % A_INFLIGHT == 0 for dil in A_DILS if seq == A_BLOCK * dil)
    col = lambda b, p: (b, p)
    blk = pl.BlockSpec((seq, LANE), col)
    return pl.pallas_call(
        functools.partial(_even_prompt_kernel, seq),
        grid=(batch, N_PAIRS),
        in_specs=[blk, blk, blk, blk,
                  pl.BlockSpec((None,) + bias.shape[1:], lambda b, p: (p, 0, 0, 0, 0)),
                  pl.BlockSpec((None, POOL_GROUP_DIM, POOL_GROUP_DIM), lambda b, p: (p, 0, 0)),
                  pl.BlockSpec((None, 1, POOL_GROUP_DIM), lambda b, p: (p, 0, 0))],
        out_specs=[blk, blk],
        out_shape=[jax.ShapeDtypeStruct((m, A_WIDTH), F32), jax.ShapeDtypeStruct((m, POOL_WIDTH), F32)],
        scratch_shapes=[pltpu.VMEM((seq, LANE), F32), pltpu.VMEM((seq, LANE), F32),
                        pltpu.VMEM((seq, LANE), F32), pltpu.VMEM((POOL_PAD + seq, LANE), F32)],
        compiler_params=_cparams(("parallel", "arbitrary")),
        name="even_prompt",
    )(q, k, v, u, bias, w_pool, scale)


def _kv_decode_kernel(aliased, kt_ref, vt_ref, q_ref, kn_ref, vn_ref, *refs):
    ko_ref, vo_ref, o_ref = refs[2:] if aliased else refs
    hist = kt_ref.shape[1]
    pos = lax.broadcasted_iota(jnp.int32, (1, hist), 1)
    dist = hist - pos
    mult = jnp.zeros((1, hist), F32)
    for window, dil in A_BRANCHES:
        member = jnp.logical_and(dist <= window, jnp.bitwise_and(dist, dil - 1) == 0)
        mult = mult + member.astype(F32)
    valid = mult > 0.0
    dist_f = dist.astype(F32)
    newest = pos == hist - 1
    n_branches = float(len(A_BRANCHES))
    for h in range(A_HEADS):
        rows = slice(h * HEAD_DIM, (h + 1) * HEAD_DIM)
        kh = kt_ref[rows, :]
        vh = vt_ref[rows, :]
        qh = q_ref[rows, :] * A_SCALE
        kn = kn_ref[rows, :]
        vn = vn_ref[rows, :]
        s = jnp.sum(kh * qh, axis=0, keepdims=True) - _alibi_slope(h) * dist_f
        s = jnp.where(valid, s, NEG)
        s_new = jnp.sum(kn * qh, axis=0, keepdims=True)
        m = jnp.maximum(jnp.max(s, axis=-1, keepdims=True), s_new)
        p = mult * jnp.exp(s - m)
        p_new = n_branches * jnp.exp(s_new - m)
        l = jnp.sum(p, axis=-1, keepdims=True) + p_new
        o_ref[rows, :] = (jnp.sum(vh * p, axis=-1, keepdims=True) + vn * p_new) / l
        ko_ref[rows, :] = jnp.where(newest, kn, pltpu.roll(kh, hist - 1, axis=1))
        vo_ref[rows, :] = jnp.where(newest, vn, pltpu.roll(vh, hist - 1, axis=1))


def _kv_decode(layer, kt, vt, q_col, kn_col, vn_col, prev):
    _, nb, width, hist = kt.shape
    blk = pl.BlockSpec((None, None, width, hist), lambda b: (layer, b, 0, 0))
    col = pl.BlockSpec((None, width, 1), lambda b: (b, 0, 0))
    in_specs = [blk, blk, col, col, col]
    args = [kt, vt, q_col, kn_col, vn_col]
    aliases = {}
    if prev is not None:
        in_specs += [pl.BlockSpec(memory_space=pl.ANY)] * 2
        aliases = {len(args): 0, len(args) + 1: 1}
        args += list(prev)
    return pl.pallas_call(
        functools.partial(_kv_decode_kernel, prev is not None),
        grid=(nb,),
        in_specs=in_specs,
        out_specs=[blk, blk, col],
        out_shape=[jax.ShapeDtypeStruct(kt.shape, F32), jax.ShapeDtypeStruct(vt.shape, F32),
                   jax.ShapeDtypeStruct((nb, width, 1), F32)],
        input_output_aliases=aliases,
        compiler_params=_cparams(("arbitrary",)),
        name="kv_decode",
    )(*args)


def _pool_decode_kernel(u_ref, hist_ref, wp_ref, sc_ref, o_ref):
    u = u_ref[...]
    for g, w in enumerate(POOL_WINDOWS):
        cols = slice(g * POOL_GROUP_DIM, (g + 1) * POOL_GROUP_DIM)
        cur = u[:, cols]
        tot = cur
        for j in range(POOL_HIST - (w - 1), POOL_HIST):
            tot = tot + hist_ref[j, :, cols]
        pooled = tot / float(w) - cur
        o_ref[:, cols] = jnp.dot(pooled.astype(BF16), wp_ref[g], preferred_element_type=F32) * sc_ref[:, cols]


def _pool_decode(u, hist_t, w_pool, scale):
    nb = u.shape[0]
    return pl.pallas_call(
        _pool_decode_kernel,
        out_shape=jax.ShapeDtypeStruct((nb, POOL_WIDTH), F32),
        compiler_params=pltpu.CompilerParams(vmem_limit_bytes=VMEM_LIMIT),
        name="pool_decode",
    )(u, hist_t, w_pool, scale)


def _gates(ba, alog_ref, dtb_ref):
    beta = jax.nn.sigmoid(ba)
    g = -jnp.exp(alog_ref[...]) * _softplus(ba + dtb_ref[...])
    return beta, g


def _l2norm(x):
    return x * lax.rsqrt(jnp.sum(x * x, axis=-1, keepdims=True) + 1e-6)


def _gated_out(o, gate, nw):
    o = o * lax.rsqrt(jnp.mean(o * o, axis=-1, keepdims=True) + RMS_EPS) * nw
    return o * _silu(gate)


def _gdn_prompt_kernel(qkv_ref, halo_ref, gate_ref, ba_ref, wc_ref, alog_ref, dtb_ref, nw_ref,
                       o_ref, sfin_ref, s_scr, xx_scr):
    step = pl.program_id(1)
    rows = GDN_ROWS

    @pl.when(step == 0)
    def _():
        s_scr[...] = jnp.zeros(s_scr.shape, F32)

    xx_scr[0:HALO, :] = jnp.where(step == 0, 0.0, halo_ref[...])
    xx_scr[HALO:HALO + rows, :] = qkv_ref[...]

    def conv(c0):
        cols = slice(c0, c0 + GDN_DK)
        acc = wc_ref[CONV_WIDTH - 1:CONV_WIDTH, cols] * xx_scr[HALO:HALO + rows, cols]
        for i in range(1, CONV_WIDTH):
            acc = acc + wc_ref[CONV_WIDTH - 1 - i:CONV_WIDTH - i, cols] * xx_scr[HALO - i:HALO - i + rows, cols]
        return _silu(acc)

    beta_all, g_all = _gates(ba_ref[...], alog_ref, dtb_ref)
    row = lax.broadcasted_iota(jnp.int32, (rows, LANE), 0)
    in_chunk = jnp.bitwise_and(row, GDN_CHUNK - 1)
    gcs = g_all
    shift = 1
    while shift < GDN_CHUNK:
        gcs = gcs + jnp.where(in_chunk >= shift, pltpu.roll(gcs, shift, axis=0), 0.0)
        shift *= 2
    glast = jnp.concatenate(
        [jnp.broadcast_to(gcs[(c + 1) * GDN_CHUNK - 1:(c + 1) * GDN_CHUNK, :], (GDN_CHUNK, LANE))
         for c in range(GDN_NC)], axis=0)
    eg_all = jnp.exp(gcs)
    ekd_all = jnp.exp(glast - gcs)
    gcs_t = gcs.T
    eglast_t = jnp.exp(gcs_t)

    ri = lax.broadcasted_iota(jnp.int32, (rows, rows), 0)
    ci = lax.broadcasted_iota(jnp.int32, (rows, rows), 1)
    same = lax.shift_right_logical(ri, GDN_CHUNK_SHIFT) == lax.shift_right_logical(ci, GDN_CHUNK_SHIFT)
    bd_tril = jnp.logical_and(same, ri >= ci)
    bd_strict = jnp.logical_and(same, ri > ci)
    eye_cat = (lax.broadcasted_iota(jnp.int32, (GDN_CHUNK, rows), 0)
               == jnp.bitwise_and(lax.broadcasted_iota(jnp.int32, (GDN_CHUNK, rows), 1), GDN_CHUNK - 1)
               ).astype(F32)

    def to_cat(x):
        out = x[0:GDN_CHUNK]
        for c in range(1, GDN_NC):
            out = out + x[c * GDN_CHUNK:(c + 1) * GDN_CHUNK]
        return out

    def to_bd(x):
        return jnp.where(same, jnp.concatenate([x] * GDN_NC, axis=0), 0.0)

    def local(h):
        gi = GDN_HEADS + h
        beta = beta_all[:, h:h + 1]
        egc = eg_all[:, gi:gi + 1]
        q = _l2norm(conv(h * GDN_DK)) * (GDN_DK ** -0.5)
        k = _l2norm(conv(GDN_QK + h * GDN_DK))
        v = conv(2 * GDN_QK + h * GDN_DK)
        kb = k * beta
        prod = _dot_nt(jnp.concatenate([kb, q], axis=0), k)
        diff = gcs[:, gi:gi + 1] - gcs_t[gi:gi + 1, :]
        decay = jnp.where(bd_tril, jnp.exp(jnp.where(bd_tril, diff, 0.0)), 0.0)
        return dict(neg_l=jnp.where(bd_strict, -(prod[:rows] * decay), 0.0),
                    a_intra=(prod[rows:] * decay).astype(BF16),
                    rhs=jnp.concatenate([v * beta, kb * egc], axis=1).astype(BF16),
                    qdec=(q * egc).astype(BF16),
                    kdec=(k * ekd_all[:, gi:gi + 1]).astype(BF16))

    for g0 in range(0, GDN_HEADS, GDN_GROUP):
        heads = list(range(g0, g0 + GDN_GROUP))
        loc = [local(h) for h in heads]
        p_cat = [to_cat(x["neg_l"]) for x in loc]
        t_cat = [eye_cat + p for p in p_cat]
        p_cat = [_dot3(p, x["neg_l"]) for p, x in zip(p_cat, loc)]
        n_levels = 6
        for level in range(1, n_levels):
            p_bd = [to_bd(p) for p in p_cat]
            if level < n_levels - 1:
                res = [_dot3(jnp.concatenate([p, t], axis=0), b) for p, t, b in zip(p_cat, t_cat, p_bd)]
                p_cat = [r[:GDN_CHUNK] for r in res]
                t_cat = [t + r[GDN_CHUNK:] for t, r in zip(t_cat, res)]
            else:
                t_cat = [t + _dot3(t, b) for t, b in zip(t_cat, p_bd)]
        uw = [jnp.dot(to_bd(t).astype(BF16), x["rhs"], preferred_element_type=F32) for t, x in zip(t_cat, loc)]
        state = [s_scr[h] for h in heads]
        v_news = [[] for _ in heads]
        q_states = [[] for _ in heads]
        for c in range(GDN_NC):
            r = slice(c * GDN_CHUNK, (c + 1) * GDN_CHUNK)
            last = (c + 1) * GDN_CHUNK - 1
            for j, h in enumerate(heads):
                gi = GDN_HEADS + h
                lhs = jnp.concatenate([uw[j][r, GDN_DK:].astype(BF16), loc[j]["qdec"][r]], axis=0)
                res = jnp.dot(lhs, state[j].astype(BF16), preferred_element_type=F32)
                v_new = uw[j][r, :GDN_DK] - res[:GDN_CHUNK]
                q_states[j].append(res[GDN_CHUNK:])
                v_news[j].append(v_new)
                state[j] = state[j] * eglast_t[gi:gi + 1, last:last + 1] + _dot_tn(loc[j]["kdec"][r], v_new)
        for j, h in enumerate(heads):
            s_scr[h] = state[j]
            o = jnp.concatenate(q_states[j], axis=0) + jnp.dot(
                loc[j]["a_intra"], jnp.concatenate(v_news[j], axis=0).astype(BF16), preferred_element_type=F32)
            cols = slice(h * GDN_DK, (h + 1) * GDN_DK)
            o_ref[:, cols] = _gated_out(o, gate_ref[:, cols], nw_ref[...])

    @pl.when(step == pl.num_programs(1) - 1)
    def _():
        sfin_ref[...] = s_scr[...]


def _gdn_prompt(qkv, gate, ba, w_conv, alog, dtb, nw, batch, seq):
    m = batch * seq
    steps = seq // GDN_ROWS
    per_halo = GDN_ROWS // HALO
    row_blk = lambda b, c: (b * steps + c, 0)
    return pl.pallas_call(
        _gdn_prompt_kernel,
        grid=(batch, steps),
        in_specs=[pl.BlockSpec((GDN_ROWS, GDN_CONV_CH), row_blk),
                  pl.BlockSpec((HALO, GDN_CONV_CH),
                               lambda b, c: (jnp.maximum((b * steps + c) * per_halo - 1, 0), 0)),
                  pl.BlockSpec((GDN_ROWS, GDN_VW), row_blk),
                  pl.BlockSpec((GDN_ROWS, LANE), row_blk),
                  _resident(w_conv.shape), _resident(alog.shape), _resident(dtb.shape), _resident(nw.shape)],
        out_specs=[pl.BlockSpec((GDN_ROWS, GDN_VW), row_blk),
                   pl.BlockSpec((None, GDN_HEADS, GDN_DK, GDN_DK), lambda b, c: (b, 0, 0, 0))],
        out_shape=[jax.ShapeDtypeStruct((m, GDN_VW), F32),
                   jax.ShapeDtypeStruct((batch, GDN_HEADS, GDN_DK, GDN_DK), F32)],
        scratch_shapes=[pltpu.VMEM((GDN_HEADS, GDN_DK, GDN_DK), F32),
                        pltpu.VMEM((HALO + GDN_ROWS, GDN_CONV_CH), F32)],
        compiler_params=_cparams(("parallel", "arbitrary")),
        name="gdn_prompt",
    )(qkv, qkv, gate, ba, w_conv, alog, dtb, nw)


def _gdn_decode_prep_kernel(x_ref, hist_ref, ba_ref, wc_ref, alog_ref, dtb_ref, q_ref, k_ref, v_ref, bg_ref):
    for h in range(3 * GDN_HEADS):
        cols = slice(h * GDN_DK, (h + 1) * GDN_DK)
        acc = wc_ref[CONV_WIDTH - 1:CONV_WIDTH, cols] * x_ref[:, cols]
        for i in range(1, CONV_WIDTH):
            acc = acc + wc_ref[CONV_WIDTH - 1 - i:CONV_WIDTH - i, cols] * hist_ref[CONV_WIDTH - 1 - i, :, cols]
        y = _silu(acc)
        if h < GDN_HEADS:
            q_ref[:, cols] = _l2norm(y) * (GDN_DK ** -0.5)
        elif h < 2 * GDN_HEADS:
            k_ref[:, slice((h - GDN_HEADS) * GDN_DK, (h - GDN_HEADS + 1) * GDN_DK)] = _l2norm(y)
        else:
            v_ref[:, slice((h - 2 * GDN_HEADS) * GDN_DK, (h - 2 * GDN_HEADS + 1) * GDN_DK)] = y
    beta, g = _gates(ba_ref[...], alog_ref, dtb_ref)
    lane = lax.broadcasted_iota(jnp.int32, beta.shape, 1)
    bg_ref[...] = jnp.where(lane < GDN_HEADS, beta, g)


def _gdn_decode_prep(x, hist_t, ba, w_conv, alog, dtb):
    nb = x.shape[0]
    head = jax.ShapeDtypeStruct((nb, GDN_QK), F32)
    return pl.pallas_call(
        _gdn_decode_prep_kernel,
        out_shape=[head, head, head, jax.ShapeDtypeStruct((nb, LANE), F32)],
        compiler_params=pltpu.CompilerParams(vmem_limit_bytes=VMEM_LIMIT),
        name="gdn_decode_prep",
    )(x, hist_t, ba, w_conv, alog, dtb)


def _gdn_decode_kernel(q_ref, k_ref, v_ref, gate_ref, beta_ref, g_ref, s_ref, nw_ref, o_ref, so_ref):
    q_all, k_all, v_all = q_ref[...], k_ref[...], v_ref[...]
    gate_all = gate_ref[...]
    beta_all = beta_ref[...]
    eg_all = jnp.exp(g_ref[...])
    qk_all = jnp.sum(q_all * k_all, axis=-1, keepdims=True)
    pad = jnp.zeros((6, GDN_DK), F32)
    outs = []
    for h in range(GDN_HEADS):
        s = s_ref[h]
        q, k, v = q_all[h:h + 1], k_all[h:h + 1], v_all[h:h + 1]
        beta, eg = beta_all[h:h + 1], eg_all[h:h + 1]
        res = _dot(jnp.concatenate([k, q, pad], axis=0), s)
        v_new = beta * (v - eg * res[0:1])
        o = eg * res[1:2] + qk_all[h:h + 1] * v_new
        k_col = jnp.broadcast_to(k, (GDN_DK, GDN_DK)).T
        so_ref[h] = s * eg + k_col * v_new
        outs.append(o)
    o_all = jnp.concatenate(outs, axis=0)
    o_ref[...] = _gated_out(o_all, gate_all, nw_ref[...])


def _gdn_decode(layer, q3, k3, v3, gate3, beta3, g3, state, nw):
    nb = q3.shape[0]
    tok = pl.BlockSpec((None, GDN_HEADS, GDN_DK), lambda b: (b, 0, 0))
    sc = pl.BlockSpec((None, GDN_HEADS, 1), lambda b: (b, 0, 0))
    return pl.pallas_call(
        _gdn_decode_kernel,
        grid=(nb,),
        in_specs=[tok, tok, tok, tok, sc, sc,
                  pl.BlockSpec((None, None, GDN_HEADS, GDN_DK, GDN_DK), lambda b: (layer, b, 0, 0, 0)),
                  _resident(nw.shape)],
        out_specs=[tok, pl.BlockSpec((None, GDN_HEADS, GDN_DK, GDN_DK), lambda b: (b, 0, 0, 0))],
        out_shape=[jax.ShapeDtypeStruct((nb, GDN_HEADS, GDN_DK), F32),
                   jax.ShapeDtypeStruct((nb, GDN_HEADS, GDN_DK, GDN_DK), F32)],
        compiler_params=_cparams(("parallel",)),
        name="gdn_decode",
    )(q3, k3, v3, gate3, beta3, g3, state, nw)


def _pad_cols(x, start, width=LANE):
    return jnp.pad(x, ((0, 0), (start, width - start - x.shape[1])))


def _positions_minor(x):
    lead = x.shape[:-3]
    n = len(lead)
    xt = x.transpose(*range(n), n + 1, n + 2, n)
    return xt.reshape(*lead, x.shape[-2] * x.shape[-1], x.shape[-3])


def _positions_major(xt, heads):
    lead = xt.shape[:-2]
    n = len(lead)
    x = xt.reshape(*lead, heads, xt.shape[-2] // heads, xt.shape[-1])
    return x.transpose(*range(n), n + 2, n, n + 1)


def kernel(x_prompt, x_sample, state_a_k, state_a_v, state_pool, state_conv, state_gdn, norm_mix, norm_ffn,
           w_in_a, w_out_a, w_pool, pool_scale, w_in_c, w_conv_c, a_log_c, dt_bias_c, norm_out_c, w_out_c,
           w_ffn_gate, w_ffn_up, w_ffn_down, norm_final):
    batch, seq, d = x_prompt.shape
    nb = x_sample.shape[0]
    depth = norm_mix.shape[0]
    tm = 512

    w_in_a_b = w_in_a.astype(BF16)
    ws_even = [[w_in_a_b[i][:, j * A_WIDTH:(j + 1) * A_WIDTH] for j in range(4)] for i in range(w_in_a.shape[0])]
    w_out_a_b = w_out_a.astype(BF16)
    w_pool_b = w_pool.astype(BF16)
    w_in_c_b = w_in_c.astype(BF16)
    ws_odd = [[w_in_c_b[i][:, :GDN_CONV_CH], w_in_c_b[i][:, GDN_CONV_CH:GDN_CONV_CH + GDN_VW],
               _pad_cols(w_in_c_b[i][:, GDN_CONV_CH + GDN_VW:], 0)] for i in range(w_in_c.shape[0])]
    w_out_c_b = w_out_c.astype(BF16)
    wg_b, wu_b, wd_b = w_ffn_gate.astype(BF16), w_ffn_up.astype(BF16), w_ffn_down.astype(BF16)
    alog_p = _pad_cols(a_log_c, GDN_HEADS)
    dtb_p = _pad_cols(dt_bias_c, GDN_HEADS)
    bias = _attn_bias()
    g_final = norm_final.reshape(1, d)

    def ffn_args(layer):
        last = layer == depth - 1
        return (norm_ffn[layer:layer + 1], wg_b[layer], wu_b[layer], wd_b[layer], g_final if last else None)

    xs = x_sample.reshape(nb, d)
    kt_hist, vt_hist = _positions_minor(state_a_k), _positions_minor(state_a_v)
    kv_out = None
    pool_s, conv_s, gdn_s = [], [], []
    for layer in range(depth):
        i = layer // 2
        g_mix = norm_mix[layer:layer + 1]
        if layer % 2 == 0:
            q, k, v, u = _norm_proj(xs, g_mix, ws_even[i], nb)
            as_col = lambda a: a.reshape(nb, A_WIDTH, 1)
            k_out, v_out, attn = _kv_decode(i, kt_hist, vt_hist, as_col(q), as_col(k), as_col(v), kv_out)
            kv_out = (k_out, v_out)
            pooled = _pool_decode(u, state_pool[i].transpose(1, 0, 2), w_pool_b[i], pool_scale[i:i + 1])
            pool_s.append(jnp.concatenate([state_pool[i][:, 1:], u[:, None, :]], axis=1))
            mixes = [attn.reshape(nb, A_WIDTH), pooled]
            wos = [w_out_a_b[i][:A_WIDTH], w_out_a_b[i][A_WIDTH:]]
        else:
            qkv, gate, ba = _norm_proj(xs, g_mix, ws_odd[i], nb)
            qn, kn, vv, bg = _gdn_decode_prep(qkv, state_conv[i].transpose(1, 0, 2), ba, w_conv_c[i],
                                              alog_p[i:i + 1], dtb_p[i:i + 1])
            shape3 = (nb, GDN_HEADS, GDN_DK)
            o3, s_new = _gdn_decode(i, qn.reshape(shape3), kn.reshape(shape3), vv.reshape(shape3),
                                    gate.reshape(shape3), bg[:, :GDN_HEADS].reshape(nb, GDN_HEADS, 1),
                                    bg[:, GDN_HEADS:2 * GDN_HEADS].reshape(nb, GDN_HEADS, 1),
                                    state_gdn, norm_out_c[i:i + 1])
            conv_s.append(jnp.concatenate([state_conv[i][:, 1:], qkv[:, None, :]], axis=1))
            gdn_s.append(s_new)
            mixes = [o3.reshape(nb, GDN_VW)]
            wos = [w_out_c_b[i]]
        xs = _mix_ffn(xs, mixes, wos, *ffn_args(layer), nb)
    y_sample = xs.reshape(nb, 1, d)
    a_k_sample = _positions_major(kv_out[0], A_HEADS)
    a_v_sample = _positions_major(kv_out[1], A_HEADS)

    xp = x_prompt.reshape(batch * seq, d)
    kt_p, vt_p, pool_p, conv_p, gdn_p = [], [], [], [], []
    for layer in range(depth):
        i = layer // 2
        g_mix = norm_mix[layer:layer + 1]
        if layer % 2 == 0:
            q, k, v, u, kt, vt = _even_in_proj(xp, g_mix, ws_even[i], batch, seq, tm)
            attn, pooled = _even_prompt(q, k, v, u, bias, w_pool_b[i],
                                        pool_scale[i].reshape(len(POOL_WINDOWS), 1, POOL_GROUP_DIM), batch, seq)
            kt_p.append(kt)
            vt_p.append(vt)
            pool_p.append(u.reshape(batch, seq, POOL_WIDTH)[:, seq - POOL_HIST:])
            mixes = [attn, pooled]
            wos = [w_out_a_b[i][:A_WIDTH], w_out_a_b[i][A_WIDTH:]]
        else:
            qkv, gate, ba = _norm_proj(xp, g_mix, ws_odd[i], tm)
            o, s_fin = _gdn_prompt(qkv, gate, ba, w_conv_c[i], alog_p[i:i + 1], dtb_p[i:i + 1],
                                   norm_out_c[i:i + 1], batch, seq)
            conv_p.append(qkv.reshape(batch, seq, GDN_CONV_CH)[:, seq - (CONV_WIDTH - 1):])
            gdn_p.append(s_fin)
            mixes = [o]
            wos = [w_out_c_b[i]]
        xp = _mix_ffn(xp, mixes, wos, *ffn_args(layer), tm)
    y_prompt = xp.reshape(batch, seq, d)
    a_k_prompt = _positions_major(jnp.stack(kt_p), A_HEADS)
    a_v_prompt = _positions_major(jnp.stack(vt_p), A_HEADS)

    return (y_prompt, y_sample, a_k_prompt, a_v_prompt, jnp.stack(pool_p), jnp.stack(conv_p),
            jnp.stack(gdn_p), a_k_sample, a_v_sample, jnp.stack(pool_s), jnp.stack(conv_s), jnp.stack(gdn_s))
```

```python
import functools

import jax
import jax.numpy as jnp
from jax import lax
from jax.experimental import pallas as pl
from jax.experimental.pallas import tpu as pltpu

F32 = jnp.float32
BF16 = jnp.bfloat16

D_MODEL = 1024
RMS_EPS = 1e-6
NEG = -1e30

HEAD_DIM = 64
A_HEADS = 8
A_WIDTH = A_HEADS * HEAD_DIM
A_BRANCHES = ((128, 1), (512, 4), (2048, 16))
A_DILS = tuple(d for _, d in A_BRANCHES)
A_KEYS = 128
A_BLOCK = 128
A_SCALE = HEAD_DIM ** -0.5
N_PAIRS = A_HEADS // 2
A_INFLIGHT = 4
KV_SLAB = 8

POOL_WINDOWS = (2, 4, 8, 16)
POOL_WIDTH = 512
POOL_GROUP_DIM = 128
POOL_HIST = 15
POOL_PAD = 16

GDN_HEADS = 8
GDN_DK = 128
GDN_QK = GDN_HEADS * GDN_DK
GDN_VW = GDN_HEADS * GDN_DK
GDN_CONV_CH = 3 * GDN_QK
CONV_WIDTH = 4
GDN_CHUNK = 64
GDN_CHUNK_SHIFT = GDN_CHUNK.bit_length() - 1
GDN_ROWS = 256
GDN_NC = GDN_ROWS // GDN_CHUNK
GDN_GROUP = 4
GDN_BASE = 8
HALO = 8

FFN_HIDDEN = 2816
FFN_CHUNK = 256
LANE = 128

VMEM_LIMIT = 56 * 1024 * 1024


def _cparams(sem):
    return pltpu.CompilerParams(dimension_semantics=sem, vmem_limit_bytes=VMEM_LIMIT)


def _resident(shape):
    nd = len(shape)
    return pl.BlockSpec(shape, lambda *_: (0,) * nd, pipeline_mode=pl.Buffered(1))


def _rms(x, g):
    ms = jnp.mean(x * x, axis=-1, keepdims=True)
    return x * lax.rsqrt(ms + RMS_EPS) * g


def _silu(x):
    return x * jax.nn.sigmoid(x)


def _softplus(x):
    return jnp.maximum(x, 0.0) + jnp.log1p(jnp.exp(-jnp.abs(x)))


def _dot(a, b):
    return jnp.dot(a.astype(BF16), b.astype(BF16), preferred_element_type=F32)


def _dot_nt(a, b):
    return lax.dot_general(a.astype(BF16), b.astype(BF16), (((1,), (1,)), ((), ())),
                           preferred_element_type=F32)


def _dot_tn(a, b):
    return lax.dot_general(a.astype(BF16), b.astype(BF16), (((0,), (0,)), ((), ())),
                           preferred_element_type=F32)


def _norm_proj_kernel(n_out, x_ref, g_ref, *refs):
    w_refs, o_refs = refs[:n_out], refs[n_out:]
    h = _rms(x_ref[...], g_ref[...]).astype(BF16)
    for w_ref, o_ref in zip(w_refs, o_refs):
        n = w_ref.shape[1]
        for c0 in range(0, n, 512):
            c1 = min(c0 + 512, n)
            o_ref[:, c0:c1] = jnp.dot(h, w_ref[:, c0:c1], preferred_element_type=F32)


def _norm_proj(x, g, ws, tm):
    m, d = x.shape
    n_out = len(ws)
    return pl.pallas_call(
        functools.partial(_norm_proj_kernel, n_out),
        grid=(m // tm,),
        in_specs=[pl.BlockSpec((tm, d), lambda i: (i, 0)), _resident((1, d))]
        + [_resident(w.shape) for w in ws],
        out_specs=[pl.BlockSpec((tm, w.shape[1]), lambda i: (i, 0)) for w in ws],
        out_shape=[jax.ShapeDtypeStruct((m, w.shape[1]), F32) for w in ws],
        compiler_params=_cparams(("parallel",)),
        name="norm_proj",
    )(x, g, *ws)


def _even_in_proj_kernel(x_ref, g_ref, wq_ref, wk_ref, wv_ref, wu_ref,
                         q_ref, k_ref, v_ref, u_ref, kt_ref, vt_ref):
    h = _rms(x_ref[...], g_ref[...]).astype(BF16)
    q_ref[...] = jnp.dot(h, wq_ref[...], preferred_element_type=F32)
    u_ref[...] = jnp.dot(h, wu_ref[...], preferred_element_type=F32)
    k = jnp.dot(h, wk_ref[...], preferred_element_type=F32)
    k_ref[...] = k
    kt_ref[...] = k.T
    v = jnp.dot(h, wv_ref[...], preferred_element_type=F32)
    v_ref[...] = v
    vt_ref[...] = v.T


def _even_in_proj(x, g, ws, batch, seq, tm):
    m, d = x.shape
    per_seq = seq // tm
    row = pl.BlockSpec((tm, A_WIDTH), lambda i: (i, 0))
    tr = pl.BlockSpec((None, A_WIDTH, tm), lambda i: (i // per_seq, 0, i % per_seq))
    flat = jax.ShapeDtypeStruct((m, A_WIDTH), F32)
    trs = jax.ShapeDtypeStruct((batch, A_WIDTH, seq), F32)
    return pl.pallas_call(
        _even_in_proj_kernel,
        grid=(m // tm,),
        in_specs=[pl.BlockSpec((tm, d), lambda i: (i, 0)), _resident((1, d))] + [_resident(w.shape) for w in ws],
        out_specs=[row, row, row, row, tr, tr],
        out_shape=[flat, flat, flat, flat, trs, trs],
        compiler_params=_cparams(("parallel",)),
        name="even_in_proj",
    )(x, g, *ws)


def _mix_ffn_kernel(n_mix, final, x_ref, *refs):
    a_refs = refs[:n_mix]
    wo_refs = refs[n_mix:2 * n_mix]
    g_ref, wg_ref, wu_ref, wd_ref = refs[2 * n_mix:2 * n_mix + 4]
    rest = refs[2 * n_mix + 4:]
    if final:
        gf_ref, o_ref, act_ref = rest
    else:
        o_ref, act_ref = rest
    x1 = x_ref[...]
    for a_ref, wo_ref in zip(a_refs, wo_refs):
        x1 = x1 + jnp.dot(a_ref[...].astype(BF16), wo_ref[...], preferred_element_type=F32)
    h = _rms(x1, g_ref[...]).astype(BF16)
    for c0 in range(0, FFN_HIDDEN, FFN_CHUNK):
        gate = jnp.dot(h, wg_ref[:, c0:c0 + FFN_CHUNK], preferred_element_type=F32)
        up = jnp.dot(h, wu_ref[:, c0:c0 + FFN_CHUNK], preferred_element_type=F32)
        act_ref[:, c0:c0 + FFN_CHUNK] = (_silu(gate) * up).astype(BF16)
    out = x1 + jnp.dot(act_ref[...], wd_ref[...], preferred_element_type=F32)
    if final:
        out = _rms(out, gf_ref[...])
    o_ref[...] = out


def _mix_ffn(x, mixes, wos, g, wg, wu, wd, g_final, tm):
    m, d = x.shape
    n_mix = len(mixes)
    final = g_final is not None
    in_specs = [pl.BlockSpec((tm, d), lambda i: (i, 0))]
    in_specs += [pl.BlockSpec((tm, a.shape[1]), lambda i: (i, 0)) for a in mixes]
    in_specs += [_resident(w.shape) for w in wos]
    in_specs += [_resident(g.shape), _resident(wg.shape), _resident(wu.shape), _resident(wd.shape)]
    args = [x, *mixes, *wos, g, wg, wu, wd]
    if final:
        in_specs.append(_resident(g_final.shape))
        args.append(g_final)
    return pl.pallas_call(
        functools.partial(_mix_ffn_kernel, n_mix, final),
        grid=(m // tm,),
        in_specs=in_specs,
        out_specs=pl.BlockSpec((tm, d), lambda i: (i, 0)),
        out_shape=jax.ShapeDtypeStruct((m, d), F32),
        scratch_shapes=[pltpu.VMEM((tm, FFN_HIDDEN), BF16)],
        compiler_params=_cparams(("parallel",)),
        name="mix_ffn",
    )(*args)


def _alibi_slope(h):
    return 2.0 ** (-8.0 * (h + 1.0) / A_HEADS)


def _attn_bias():
    rel = jnp.arange(A_BLOCK)[:, None] + A_BLOCK - jnp.arange(2 * A_BLOCK)[None, :]
    band = (rel >= 0) & (rel <= A_KEYS)
    slopes = jnp.exp2(-8.0 * (jnp.arange(A_HEADS, dtype=F32) + 1.0) / A_HEADS)
    dils = jnp.asarray(A_DILS, F32)
    bias = -slopes[None, :, None, None] * (rel.astype(F32)[None, None] * dils[:, None, None, None])
    bias = jnp.where(band[None, None], bias, NEG)
    bias = bias.reshape(len(A_DILS), N_PAIRS, 2, A_BLOCK, 2 * A_BLOCK)
    return bias.transpose(1, 0, 2, 3, 4)


def _attn_scores(q, kk, vv, bias_ref, d_idx, first):
    lane = lax.broadcasted_iota(jnp.int32, (A_BLOCK, LANE), 1)
    lo = lane < HEAD_DIM
    kb = kk.astype(BF16)
    vb = vv.astype(BF16)
    stats = []
    for h in range(2):
        qm = jnp.where(lo if h == 0 else jnp.logical_not(lo), q * A_SCALE, 0.0)
        s = _dot_nt(qm, kb)
        if first:
            s = s + bias_ref[d_idx, h, :, A_BLOCK:]
        else:
            s = s + bias_ref[d_idx, h]
        m = jnp.max(s, axis=-1, keepdims=True)
        p = jnp.exp(s - m)
        l = jnp.sum(p, axis=-1, keepdims=True)
        pv = jnp.dot(p.astype(BF16), vb, preferred_element_type=F32)
        stats.append((m, l, pv))
    m = jnp.where(lo, stats[0][0], stats[1][0])
    l = jnp.where(lo, stats[0][1], stats[1][1])
    acc = jnp.where(lo, stats[0][2], stats[1][2])
    return m, l, acc


def _even_prompt_kernel(seq, q_ref, k_ref, v_ref, u_ref, bias_ref, wp_ref, sc_ref,
                        attn_ref, pool_ref, m_scr, l_scr, acc_scr, pad_scr):
    def rows(dil, start):
        if dil == 1:
            return pl.ds(start, A_BLOCK)
        return pl.ds(start, A_BLOCK, stride=dil)

    def compute(d_idx, dil, start, first):
        cur = rows(dil, start)
        q = q_ref[cur, :]
        if first:
            kk, vv = k_ref[cur, :], v_ref[cur, :]
        elif dil == 1:
            win = pl.ds(start - A_BLOCK, 2 * A_BLOCK)
            kk, vv = k_ref[win, :], v_ref[win, :]
        else:
            prev = rows(dil, start - A_BLOCK * dil)
            kk = jnp.concatenate([k_ref[prev, :], k_ref[cur, :]], axis=0)
            vv = jnp.concatenate([v_ref[prev, :], v_ref[cur, :]], axis=0)
        m, l, acc = _attn_scores(q, kk, vv, bias_ref, d_idx, first)
        if d_idx != first_branch:
            m0, l0, a0 = m_scr[cur, :], l_scr[cur, :], acc_scr[cur, :]
            mx = jnp.maximum(m0, m)
            w0 = jnp.exp(m0 - mx)
            w1 = jnp.exp(m - mx)
            m, l, acc = mx, w0 * l0 + w1 * l, w0 * a0 + w1 * acc
        return cur, m, l, acc

    def blocks(d_idx, dil, todo):
        done = [compute(d_idx, dil, start, first) for start, first in todo]
        for cur, m, l, acc in done:
            m_scr[cur, :] = m
            l_scr[cur, :] = l
            acc_scr[cur, :] = acc

    first_branch = len(A_DILS) - 1
    for d_idx, dil in reversed(list(enumerate(A_DILS))):
        span = A_BLOCK * dil
        n_blocks = seq // span
        if dil == 1:
            blocks(d_idx, dil, [(0, True)])
            group = 3

            def later(g, c, d_idx=d_idx, dil=dil, span=span, group=group):
                blocks(d_idx, dil, [((1 + g * group + j) * span, False) for j in range(group)])
                return c

            lax.fori_loop(0, (n_blocks - 1) // group, later, 0)
        elif n_blocks > 1:
            def residue(r, c, d_idx=d_idx, dil=dil, span=span, n_blocks=n_blocks):
                blocks(d_idx, dil, [(r, True)] + [(n * span + r, False) for n in range(1, n_blocks)])
                return c

            lax.fori_loop(0, dil, residue, 0)
        else:
            def residues(g, c, d_idx=d_idx, dil=dil):
                blocks(d_idx, dil, [(g * A_INFLIGHT + j, True) for j in range(A_INFLIGHT)])
                return c

            lax.fori_loop(0, dil // A_INFLIGHT, residues, 0)

    attn_ref[...] = acc_scr[...] / l_scr[...]

    p_idx = pl.program_id(1)
    u = u_ref[...]
    pad_scr[0:POOL_PAD, :] = jnp.zeros((POOL_PAD, LANE), F32)
    s = u
    shift = 1
    for level in range(len(POOL_WINDOWS)):
        pad_scr[POOL_PAD:POOL_PAD + seq, :] = s
        nxt = s + pad_scr[POOL_PAD - shift:POOL_PAD - shift + seq, :]
        s = nxt if level == 0 else jnp.where(p_idx >= level, nxt, s)
        shift *= 2
    pos = lax.broadcasted_iota(jnp.int32, (seq, 1), 0)
    window = lax.shift_left(jnp.int32(2), p_idx)
    cnt = jnp.minimum(pos + 1, window).astype(F32)
    pooled = s / cnt - u
    pool_ref[...] = jnp.dot(pooled.astype(BF16), wp_ref[...], preferred_element_type=F32) * sc_ref[...]


def _even_prompt(q, k, v, u, bias, w_pool, scale, batch, seq):
    m = batch * seq
    assert (seq // A_BLOCK - 1) % 3 == 0 and seq % (A_BLOCK * max(A_DILS)) == 0
    assert all(dil % A_INFLIGHT == 0 for dil in A_DILS if seq == A_BLOCK * dil)
    col = lambda b, p: (b, p)
    blk = pl.BlockSpec((seq, LANE), col)
    return pl.pallas_call(
        functools.partial(_even_prompt_kernel, seq),
        grid=(batch, N_PAIRS),
        in_specs=[blk, blk, blk, blk,
                  pl.BlockSpec((None,) + bias.shape[1:], lambda b, p: (p, 0, 0, 0, 0)),
                  pl.BlockSpec((None, POOL_GROUP_DIM, POOL_GROUP_DIM), lambda b, p: (p, 0, 0)),
                  pl.BlockSpec((None, 1, POOL_GROUP_DIM), lambda b, p: (p, 0, 0))],
        out_specs=[blk, blk],
        out_shape=[jax.ShapeDtypeStruct((m, A_WIDTH), F32), jax.ShapeDtypeStruct((m, POOL_WIDTH), F32)],
        scratch_shapes=[pltpu.VMEM((seq, LANE), F32), pltpu.VMEM((seq, LANE), F32),
                        pltpu.VMEM((seq, LANE), F32), pltpu.VMEM((POOL_PAD + seq, LANE), F32)],
        compiler_params=_cparams(("parallel", "arbitrary")),
        name="even_prompt",
    )(q, k, v, u, bias, w_pool, scale)


def _kv_decode_kernel(aliased, kt_ref, vt_ref, q_ref, kn_ref, vn_ref, *refs):
    ko_ref, vo_ref, o_ref = refs[2:] if aliased else refs
    hist = kt_ref.shape[1]
    pos = lax.broadcasted_iota(jnp.int32, (1, hist), 1)
    dist = hist - pos
    mult = jnp.zeros((1, hist), F32)
    for window, dil in A_BRANCHES:
        member = jnp.logical_and(dist <= window, jnp.bitwise_and(dist, dil - 1) == 0)
        mult = mult + member.astype(F32)
    valid = mult > 0.0
    dist_f = dist.astype(F32)
    n_branches = float(len(A_BRANCHES))
    for h in range(A_HEADS):
        col = slice(h, h + 1)
        slabs = [(slice(d, d + KV_SLAB), slice(h * HEAD_DIM + d, h * HEAD_DIM + d + KV_SLAB))
                 for d in range(0, HEAD_DIM, KV_SLAB)]
        qh = q_ref[:, col] * A_SCALE
        part = None
        for dsl, rsl in slabs:
            t = kt_ref[rsl, :] * qh[dsl]
            part = t if part is None else part + t
        s = jnp.sum(part, axis=0, keepdims=True) - _alibi_slope(h) * dist_f
        s = jnp.where(valid, s, NEG)
        s_new = jnp.sum(kn_ref[:, col] * qh, axis=0, keepdims=True)
        m = jnp.maximum(jnp.max(s, axis=-1, keepdims=True), s_new)
        p = mult * jnp.exp(s - m)
        p_new = n_branches * jnp.exp(s_new - m)
        inv_l = 1.0 / (jnp.sum(p, axis=-1, keepdims=True) + p_new)
        for dsl, rsl in slabs:
            pv = jnp.sum(vt_ref[rsl, :] * p, axis=-1, keepdims=True)
            o_ref[dsl, col] = (pv + vn_ref[dsl, col] * p_new) * inv_l

    last = slice(hist - LANE, hist)
    newest = lax.broadcasted_iota(jnp.int32, (KV_SLAB, LANE), 1) == LANE - 1
    for src_ref, new_ref, dst_ref in ((kt_ref, kn_ref, ko_ref), (vt_ref, vn_ref, vo_ref)):
        for h in range(A_HEADS):
            for d in range(0, HEAD_DIM, KV_SLAB):
                rsl = slice(h * HEAD_DIM + d, h * HEAD_DIM + d + KV_SLAB)
                rolled = pltpu.roll(src_ref[rsl, :], hist - 1, axis=1)
                dst_ref[rsl, 0:hist - LANE] = rolled[:, 0:hist - LANE]
                dst_ref[rsl, last] = jnp.where(newest, new_ref[d:d + KV_SLAB, h:h + 1], rolled[:, last])


def _kv_decode(layer, kt, vt, q_col, kn_col, vn_col, prev):
    _, nb, width, hist = kt.shape
    blk = pl.BlockSpec((None, None, width, hist), lambda b: (layer, b, 0, 0))
    col = pl.BlockSpec((None, HEAD_DIM, A_HEADS), lambda b: (b, 0, 0))
    in_specs = [blk, blk, col, col, col]
    args = [kt, vt, q_col, kn_col, vn_col]
    aliases = {}
    if prev is not None:
        in_specs += [pl.BlockSpec(memory_space=pl.ANY)] * 2
        aliases = {len(args): 0, len(args) + 1: 1}
        args += list(prev)
    return pl.pallas_call(
        functools.partial(_kv_decode_kernel, prev is not None),
        grid=(nb,),
        in_specs=in_specs,
        out_specs=[blk, blk, col],
        out_shape=[jax.ShapeDtypeStruct(kt.shape, F32), jax.ShapeDtypeStruct(vt.shape, F32),
                   jax.ShapeDtypeStruct((nb, HEAD_DIM, A_HEADS), F32)],
        input_output_aliases=aliases,
        compiler_params=_cparams(("arbitrary",)),
        name="kv_decode",
    )(*args)


def _pool_decode_kernel(u_ref, hist_ref, wp_ref, sc_ref, o_ref):
    u = u_ref[...]
    for g, w in enumerate(POOL_WINDOWS):
        cols = slice(g * POOL_GROUP_DIM, (g + 1) * POOL_GROUP_DIM)
        cur = u[:, cols]
        tot = cur
        for j in range(POOL_HIST - (w - 1), POOL_HIST):
            tot = tot + hist_ref[j, :, cols]
        pooled = tot / float(w) - cur
        o_ref[:, cols] = jnp.dot(pooled.astype(BF16), wp_ref[g], preferred_element_type=F32) * sc_ref[:, cols]


def _pool_decode(u, hist_t, w_pool, scale):
    nb = u.shape[0]
    return pl.pallas_call(
        _pool_decode_kernel,
        out_shape=jax.ShapeDtypeStruct((nb, POOL_WIDTH), F32),
        compiler_params=pltpu.CompilerParams(vmem_limit_bytes=VMEM_LIMIT),
        name="pool_decode",
    )(u, hist_t, w_pool, scale)


def _gates(ba, alog_ref, dtb_ref):
    beta = jax.nn.sigmoid(ba)
    g = -jnp.exp(alog_ref[...]) * _softplus(ba + dtb_ref[...])
    return beta, g


def _l2norm(x):
    return x * lax.rsqrt(jnp.sum(x * x, axis=-1, keepdims=True) + 1e-6)


def _gated_out(o, gate, nw):
    o = o * lax.rsqrt(jnp.mean(o * o, axis=-1, keepdims=True) + RMS_EPS) * nw
    return o * _silu(gate)


def _gdn_prompt_kernel(qkv_ref, halo_ref, gate_ref, ba_ref, wc_ref, alog_ref, dtb_ref, nw_ref,
                       o_ref, sfin_ref, s_scr, xx_scr):
    step = pl.program_id(1)
    rows = GDN_ROWS

    @pl.when(step == 0)
    def _():
        s_scr[...] = jnp.zeros(s_scr.shape, F32)

    xx_scr[0:HALO, :] = jnp.where(step == 0, 0.0, halo_ref[...])
    xx_scr[HALO:HALO + rows, :] = qkv_ref[...]

    def conv(c0):
        cols = slice(c0, c0 + GDN_DK)
        acc = wc_ref[CONV_WIDTH - 1:CONV_WIDTH, cols] * xx_scr[HALO:HALO + rows, cols]
        for i in range(1, CONV_WIDTH):
            acc = acc + wc_ref[CONV_WIDTH - 1 - i:CONV_WIDTH - i, cols] * xx_scr[HALO - i:HALO - i + rows, cols]
        return _silu(acc)

    beta_all, g_all = _gates(ba_ref[...], alog_ref, dtb_ref)
    row = lax.broadcasted_iota(jnp.int32, (rows, LANE), 0)
    in_chunk = jnp.bitwise_and(row, GDN_CHUNK - 1)
    gcs = g_all
    shift = 1
    while shift < GDN_CHUNK:
        gcs = gcs + jnp.where(in_chunk >= shift, pltpu.roll(gcs, shift, axis=0), 0.0)
        shift *= 2
    glast = jnp.concatenate(
        [jnp.broadcast_to(gcs[(c + 1) * GDN_CHUNK - 1:(c + 1) * GDN_CHUNK, :], (GDN_CHUNK, LANE))
         for c in range(GDN_NC)], axis=0)
    eg_all = jnp.exp(gcs)
    ekd_all = jnp.exp(glast - gcs)
    gcs_t = gcs.T
    eglast_t = jnp.exp(gcs_t)

    ri = lax.broadcasted_iota(jnp.int32, (rows, rows), 0)
    ci = lax.broadcasted_iota(jnp.int32, (rows, rows), 1)
    same = lax.shift_right_logical(ri, GDN_CHUNK_SHIFT) == lax.shift_right_logical(ci, GDN_CHUNK_SHIFT)
    bd_tril = jnp.logical_and(same, ri >= ci)
    bd_strict = jnp.logical_and(same, ri > ci)
    eye_cat = (lax.broadcasted_iota(jnp.int32, (GDN_CHUNK, rows), 0)
               == jnp.bitwise_and(lax.broadcasted_iota(jnp.int32, (GDN_CHUNK, rows), 1), GDN_CHUNK - 1)
               ).astype(F32)

    def block_of(x, size):
        return lax.shift_right_logical(x, size.bit_length() - 1)

    m_same = same.astype(BF16)
    m_base = (block_of(ri, GDN_BASE) == block_of(ci, GDN_BASE)).astype(BF16)
    m_merge = []
    size = GDN_BASE
    while size < GDN_CHUNK:
        pair = block_of(ri, 2 * size) == block_of(ci, 2 * size)
        lower_left = jnp.logical_and(jnp.bitwise_and(block_of(ri, size), 1) == 1,
                                     jnp.bitwise_and(block_of(ci, size), 1) == 0)
        m_merge.append(jnp.logical_and(pair, lower_left).astype(BF16))
        size *= 2

    def to_cat(x):
        out = x[0:GDN_CHUNK]
        for c in range(1, GDN_NC):
            out = out + x[c * GDN_CHUNK:(c + 1) * GDN_CHUNK]
        return out

    def to_bd(x):
        return jnp.concatenate([x.astype(BF16)] * GDN_NC, axis=0) * m_same

    def mm(a, b):
        return jnp.dot(a.astype(BF16), b, preferred_element_type=F32)

    def local(h):
        gi = GDN_HEADS + h
        beta = beta_all[:, h:h + 1]
        egc = eg_all[:, gi:gi + 1]
        q = _l2norm(conv(h * GDN_DK)) * (GDN_DK ** -0.5)
        k = _l2norm(conv(GDN_QK + h * GDN_DK))
        v = conv(2 * GDN_QK + h * GDN_DK)
        kb = k * beta
        prod = _dot_nt(jnp.concatenate([kb, q], axis=0), k)
        diff = gcs[:, gi:gi + 1] - gcs_t[gi:gi + 1, :]
        decay = jnp.where(bd_tril, jnp.exp(jnp.where(bd_tril, diff, 0.0)), 0.0)
        return dict(neg_l=jnp.where(bd_strict, -(prod[:rows] * decay), 0.0),
                    a_intra=(prod[rows:] * decay).astype(BF16),
                    rhs=jnp.concatenate([v * beta, kb * egc], axis=1).astype(BF16),
                    qdec=(q * egc).astype(BF16),
                    kdec=(k * ekd_all[:, gi:gi + 1]).astype(BF16))

    for g0 in range(0, GDN_HEADS, GDN_GROUP):
        heads = list(range(g0, g0 + GDN_GROUP))
        loc = [local(h) for h in heads]
        neg_l = [x["neg_l"].astype(BF16) for x in loc]
        p_bd = [n * m_base for n in neg_l]
        p_cat = [to_cat(p) for p in p_bd]
        inv = [eye_cat + p.astype(F32) for p in p_cat]
        p_cat = [mm(p, b) for p, b in zip(p_cat, p_bd)]
        power = 2
        while 2 * power < GDN_BASE:
            res = [mm(jnp.concatenate([p, t], axis=0), to_bd(p)) for p, t in zip(p_cat, inv)]
            p_cat = [r[:GDN_CHUNK] for r in res]
            inv = [t + r[GDN_CHUNK:] for t, r in zip(inv, res)]
            power *= 2
        inv = [t + mm(t, to_bd(p)) for p, t in zip(p_cat, inv)]
        for m_off in m_merge:
            half = [mm(t, n * m_off) for t, n in zip(inv, neg_l)]
            inv = [t + mm(y, to_bd(t)) for t, y in zip(inv, half)]
        uw = [jnp.dot(to_bd(t), x["rhs"], preferred_element_type=F32) for t, x in zip(inv, loc)]
        state = [s_scr[h] for h in heads]
        v_news = [[] for _ in heads]
        q_states = [[] for _ in heads]
        for c in range(GDN_NC):
            r = slice(c * GDN_CHUNK, (c + 1) * GDN_CHUNK)
            last = (c + 1) * GDN_CHUNK - 1
            for j, h in enumerate(heads):
                gi = GDN_HEADS + h
                lhs = jnp.concatenate([uw[j][r, GDN_DK:].astype(BF16), loc[j]["qdec"][r]], axis=0)
                res = jnp.dot(lhs, state[j].astype(BF16), preferred_element_type=F32)
                v_new = uw[j][r, :GDN_DK] - res[:GDN_CHUNK]
                q_states[j].append(res[GDN_CHUNK:])
                v_news[j].append(v_new)
                state[j] = state[j] * eglast_t[gi:gi + 1, last:last + 1] + _dot_tn(loc[j]["kdec"][r], v_new)
        for j, h in enumerate(heads):
            s_scr[h] = state[j]
            o = jnp.concatenate(q_states[j], axis=0) + jnp.dot(
                loc[j]["a_intra"], jnp.concatenate(v_news[j], axis=0).astype(BF16), preferred_element_type=F32)
            cols = slice(h * GDN_DK, (h + 1) * GDN_DK)
            o_ref[:, cols] = _gated_out(o, gate_ref[:, cols], nw_ref[...])

    @pl.when(step == pl.num_programs(1) - 1)
    def _():
        sfin_ref[...] = s_scr[...]


def _gdn_prompt(qkv, gate, ba, w_conv, alog, dtb, nw, batch, seq):
    m = batch * seq
    steps = seq // GDN_ROWS
    per_halo = GDN_ROWS // HALO
    row_blk = lambda b, c: (b * steps + c, 0)
    return pl.pallas_call(
        _gdn_prompt_kernel,
        grid=(batch, steps),
        in_specs=[pl.BlockSpec((GDN_ROWS, GDN_CONV_CH), row_blk),
                  pl.BlockSpec((HALO, GDN_CONV_CH),
                               lambda b, c: (jnp.maximum((b * steps + c) * per_halo - 1, 0), 0)),
                  pl.BlockSpec((GDN_ROWS, GDN_VW), row_blk),
                  pl.BlockSpec((GDN_ROWS, LANE), row_blk),
                  _resident(w_conv.shape), _resident(alog.shape), _resident(dtb.shape), _resident(nw.shape)],
        out_specs=[pl.BlockSpec((GDN_ROWS, GDN_VW), row_blk),
                   pl.BlockSpec((None, GDN_HEADS, GDN_DK, GDN_DK), lambda b, c: (b, 0, 0, 0))],
        out_shape=[jax.ShapeDtypeStruct((m, GDN_VW), F32),
                   jax.ShapeDtypeStruct((batch, GDN_HEADS, GDN_DK, GDN_DK), F32)],
        scratch_shapes=[pltpu.VMEM((GDN_HEADS, GDN_DK, GDN_DK), F32),
                        pltpu.VMEM((HALO + GDN_ROWS, GDN_CONV_CH), F32)],
        compiler_params=_cparams(("parallel", "arbitrary")),
        name="gdn_prompt",
    )(qkv, qkv, gate, ba, w_conv, alog, dtb, nw)


def _gdn_decode_prep_kernel(x_ref, hist_ref, ba_ref, wc_ref, alog_ref, dtb_ref, q_ref, k_ref, v_ref, bg_ref):
    for h in range(3 * GDN_HEADS):
        cols = slice(h * GDN_DK, (h + 1) * GDN_DK)
        acc = wc_ref[CONV_WIDTH - 1:CONV_WIDTH, cols] * x_ref[:, cols]
        for i in range(1, CONV_WIDTH):
            acc = acc + wc_ref[CONV_WIDTH - 1 - i:CONV_WIDTH - i, cols] * hist_ref[CONV_WIDTH - 1 - i, :, cols]
        y = _silu(acc)
        if h < GDN_HEADS:
            q_ref[:, cols] = _l2norm(y) * (GDN_DK ** -0.5)
        elif h < 2 * GDN_HEADS:
            k_ref[:, slice((h - GDN_HEADS) * GDN_DK, (h - GDN_HEADS + 1) * GDN_DK)] = _l2norm(y)
        else:
            v_ref[:, slice((h - 2 * GDN_HEADS) * GDN_DK, (h - 2 * GDN_HEADS + 1) * GDN_DK)] = y
    beta, g = _gates(ba_ref[...], alog_ref, dtb_ref)
    lane = lax.broadcasted_iota(jnp.int32, beta.shape, 1)
    bg_ref[...] = jnp.where(lane < GDN_HEADS, beta, g)


def _gdn_decode_prep(x, hist_t, ba, w_conv, alog, dtb):
    nb = x.shape[0]
    head = jax.ShapeDtypeStruct((nb, GDN_QK), F32)
    return pl.pallas_call(
        _gdn_decode_prep_kernel,
        out_shape=[head, head, head, jax.ShapeDtypeStruct((nb, LANE), F32)],
        compiler_params=pltpu.CompilerParams(vmem_limit_bytes=VMEM_LIMIT),
        name="gdn_decode_prep",
    )(x, hist_t, ba, w_conv, alog, dtb)


def _gdn_decode_kernel(q_ref, k_ref, v_ref, gate_ref, beta_ref, g_ref, s_ref, nw_ref, o_ref, so_ref):
    q_all, k_all, v_all = q_ref[...], k_ref[...], v_ref[...]
    gate_all = gate_ref[...]
    beta_all = beta_ref[...]
    eg_all = jnp.exp(g_ref[...])
    qk_all = jnp.sum(q_all * k_all, axis=-1, keepdims=True)
    pad = jnp.zeros((6, GDN_DK), F32)
    outs = []
    for h in range(GDN_HEADS):
        s = s_ref[h]
        q, k, v = q_all[h:h + 1], k_all[h:h + 1], v_all[h:h + 1]
        beta, eg = beta_all[h:h + 1], eg_all[h:h + 1]
        res = _dot(jnp.concatenate([k, q, pad], axis=0), s)
        v_new = beta * (v - eg * res[0:1])
        o = eg * res[1:2] + qk_all[h:h + 1] * v_new
        k_col = jnp.broadcast_to(k, (GDN_DK, GDN_DK)).T
        so_ref[h] = s * eg + k_col * v_new
        outs.append(o)
    o_all = jnp.concatenate(outs, axis=0)
    o_ref[...] = _gated_out(o_all, gate_all, nw_ref[...])


def _gdn_decode(layer, q3, k3, v3, gate3, beta3, g3, state, nw):
    nb = q3.shape[0]
    tok = pl.BlockSpec((None, GDN_HEADS, GDN_DK), lambda b: (b, 0, 0))
    sc = pl.BlockSpec((None, GDN_HEADS, 1), lambda b: (b, 0, 0))
    return pl.pallas_call(
        _gdn_decode_kernel,
        grid=(nb,),
        in_specs=[tok, tok, tok, tok, sc, sc,
                  pl.BlockSpec((None, None, GDN_HEADS, GDN_DK, GDN_DK), lambda b: (layer, b, 0, 0, 0)),
                  _resident(nw.shape)],
        out_specs=[tok, pl.BlockSpec((None, GDN_HEADS, GDN_DK, GDN_DK), lambda b: (b, 0, 0, 0))],
        out_shape=[jax.ShapeDtypeStruct((nb, GDN_HEADS, GDN_DK), F32),
                   jax.ShapeDtypeStruct((nb, GDN_HEADS, GDN_DK, GDN_DK), F32)],
        compiler_params=_cparams(("parallel",)),
        name="gdn_decode",
    )(q3, k3, v3, gate3, beta3, g3, state, nw)


def _pad_cols(x, start, width=LANE):
    return jnp.pad(x, ((0, 0), (start, width - start - x.shape[1])))


def _positions_minor(x):
    lead = x.shape[:-3]
    n = len(lead)
    xt = x.transpose(*range(n), n + 1, n + 2, n)
    return xt.reshape(*lead, x.shape[-2] * x.shape[-1], x.shape[-3])


def _positions_major(xt, heads):
    lead = xt.shape[:-2]
    n = len(lead)
    x = xt.reshape(*lead, heads, xt.shape[-2] // heads, xt.shape[-1])
    return x.transpose(*range(n), n + 2, n, n + 1)


def kernel(x_prompt, x_sample, state_a_k, state_a_v, state_pool, state_conv, state_gdn, norm_mix, norm_ffn,
           w_in_a, w_out_a, w_pool, pool_scale, w_in_c, w_conv_c, a_log_c, dt_bias_c, norm_out_c, w_out_c,
           w_ffn_gate, w_ffn_up, w_ffn_down, norm_final):
    batch, seq, d = x_prompt.shape
    nb = x_sample.shape[0]
    depth = norm_mix.shape[0]
    tm = 512

    w_in_a_b = w_in_a.astype(BF16)
    ws_even = [[w_in_a_b[i][:, j * A_WIDTH:(j + 1) * A_WIDTH] for j in range(4)] for i in range(w_in_a.shape[0])]
    w_out_a_b = w_out_a.astype(BF16)
    w_pool_b = w_pool.astype(BF16)
    w_in_c_b = w_in_c.astype(BF16)
    ws_odd = [[w_in_c_b[i][:, :GDN_CONV_CH], w_in_c_b[i][:, GDN_CONV_CH:GDN_CONV_CH + GDN_VW],
               _pad_cols(w_in_c_b[i][:, GDN_CONV_CH + GDN_VW:], 0)] for i in range(w_in_c.shape[0])]
    w_out_c_b = w_out_c.astype(BF16)
    wg_b, wu_b, wd_b = w_ffn_gate.astype(BF16), w_ffn_up.astype(BF16), w_ffn_down.astype(BF16)
    alog_p = _pad_cols(a_log_c, GDN_HEADS)
    dtb_p = _pad_cols(dt_bias_c, GDN_HEADS)
    bias = _attn_bias()
    g_final = norm_final.reshape(1, d)

    def ffn_args(layer):
        last = layer == depth - 1
        return (norm_ffn[layer:layer + 1], wg_b[layer], wu_b[layer], wd_b[layer], g_final if last else None)

    xs = x_sample.reshape(nb, d)
    kt_hist, vt_hist = _positions_minor(state_a_k), _positions_minor(state_a_v)
    kv_out = None
    pool_s, conv_s, gdn_s = [], [], []
    for layer in range(depth):
        i = layer // 2
        g_mix = norm_mix[layer:layer + 1]
        if layer % 2 == 0:
            q, k, v, u = _norm_proj(xs, g_mix, ws_even[i], nb)
            as_col = lambda a: a.reshape(nb, A_HEADS, HEAD_DIM).transpose(0, 2, 1)
            k_out, v_out, attn = _kv_decode(i, kt_hist, vt_hist, as_col(q), as_col(k), as_col(v), kv_out)
            kv_out = (k_out, v_out)
            pooled = _pool_decode(u, state_pool[i].transpose(1, 0, 2), w_pool_b[i], pool_scale[i:i + 1])
            pool_s.append(jnp.concatenate([state_pool[i][:, 1:], u[:, None, :]], axis=1))
            mixes = [attn.transpose(0, 2, 1).reshape(nb, A_WIDTH), pooled]
            wos = [w_out_a_b[i][:A_WIDTH], w_out_a_b[i][A_WIDTH:]]
        else:
            qkv, gate, ba = _norm_proj(xs, g_mix, ws_odd[i], nb)
            qn, kn, vv, bg = _gdn_decode_prep(qkv, state_conv[i].transpose(1, 0, 2), ba, w_conv_c[i],
                                              alog_p[i:i + 1], dtb_p[i:i + 1])
            shape3 = (nb, GDN_HEADS, GDN_DK)
            o3, s_new = _gdn_decode(i, qn.reshape(shape3), kn.reshape(shape3), vv.reshape(shape3),
                                    gate.reshape(shape3), bg[:, :GDN_HEADS].reshape(nb, GDN_HEADS, 1),
                                    bg[:, GDN_HEADS:2 * GDN_HEADS].reshape(nb, GDN_HEADS, 1),
                                    state_gdn, norm_out_c[i:i + 1])
            conv_s.append(jnp.concatenate([state_conv[i][:, 1:], qkv[:, None, :]], axis=1))
            gdn_s.append(s_new)
            mixes = [o3.reshape(nb, GDN_VW)]
            wos = [w_out_c_b[i]]
        xs = _mix_ffn(xs, mixes, wos, *ffn_args(layer), nb)
    y_sample = xs.reshape(nb, 1, d)
    a_k_sample = _positions_major(kv_out[0], A_HEADS)
    a_v_sample = _positions_major(kv_out[1], A_HEADS)

    xp = x_prompt.reshape(batch * seq, d)
    kt_p, vt_p, pool_p, conv_p, gdn_p = [], [], [], [], []
    for layer in range(depth):
        i = layer // 2
        g_mix = norm_mix[layer:layer + 1]
        if layer % 2 == 0:
            q, k, v, u, kt, vt = _even_in_proj(xp, g_mix, ws_even[i], batch, seq, tm)
            attn, pooled = _even_prompt(q, k, v, u, bias, w_pool_b[i],
                                        pool_scale[i].reshape(len(POOL_WINDOWS), 1, POOL_GROUP_DIM), batch, seq)
            kt_p.append(kt)
            vt_p.append(vt)
            pool_p.append(u.reshape(batch, seq, POOL_WIDTH)[:, seq - POOL_HIST:])
            mixes = [attn, pooled]
            wos = [w_out_a_b[i][:A_WIDTH], w_out_a_b[i][A_WIDTH:]]
        else:
            qkv, gate, ba = _norm_proj(xp, g_mix, ws_odd[i], tm)
            o, s_fin = _gdn_prompt(qkv, gate, ba, w_conv_c[i], alog_p[i:i + 1], dtb_p[i:i + 1],
                                   norm_out_c[i:i + 1], batch, seq)
            conv_p.append(qkv.reshape(batch, seq, GDN_CONV_CH)[:, seq - (CONV_WIDTH - 1):])
            gdn_p.append(s_fin)
            mixes = [o]
            wos = [w_out_c_b[i]]
        xp = _mix_ffn(xp, mixes, wos, *ffn_args(layer), tm)
    y_prompt = xp.reshape(batch, seq, d)
    a_k_prompt = _positions_major(jnp.stack(kt_p), A_HEADS)
    a_v_prompt = _positions_major(jnp.stack(vt_p), A_HEADS)

    return (y_prompt, y_sample, a_k_prompt, a_v_prompt, jnp.stack(pool_p), jnp.stack(conv_p),
            jnp.stack(gdn_p), a_k_sample, a_v_sample, jnp.stack(pool_s), jnp.stack(conv_s), jnp.stack(gdn_s))
```

```python
import functools

import jax
import jax.numpy as jnp
from jax import lax
from jax.experimental import pallas as pl
from jax.experimental.pallas import tpu as pltpu

F32 = jnp.float32
BF16 = jnp.bfloat16

D_MODEL = 1024
RMS_EPS = 1e-6
NEG = -1e30

HEAD_DIM = 64
A_HEADS = 8
A_WIDTH = A_HEADS * HEAD_DIM
A_BRANCHES = ((128, 1), (512, 4), (2048, 16))
A_DILS = tuple(d for _, d in A_BRANCHES)
A_KEYS = 128
A_BLOCK = 128
A_SCALE = HEAD_DIM ** -0.5
N_PAIRS = A_HEADS // 2
A_INFLIGHT = 4
KV_SLAB = 8

POOL_WINDOWS = (2, 4, 8, 16)
POOL_WIDTH = 512
POOL_GROUP_DIM = 128
POOL_HIST = 15
POOL_PAD = 16

GDN_HEADS = 8
GDN_DK = 128
GDN_QK = GDN_HEADS * GDN_DK
GDN_VW = GDN_HEADS * GDN_DK
GDN_CONV_CH = 3 * GDN_QK
CONV_WIDTH = 4
GDN_CHUNK = 64
GDN_CHUNK_SHIFT = GDN_CHUNK.bit_length() - 1
GDN_ROWS = 256
GDN_NC = GDN_ROWS // GDN_CHUNK
GDN_GROUP = 4
GDN_BASE = 8
HALO = 8
GDN_DECODE_TOKENS = 4

FFN_HIDDEN = 2816
FFN_CHUNK = 256
LANE = 128

VMEM_LIMIT = 56 * 1024 * 1024


def _cparams(sem):
    return pltpu.CompilerParams(dimension_semantics=sem, vmem_limit_bytes=VMEM_LIMIT)


def _resident(shape):
    nd = len(shape)
    return pl.BlockSpec(shape, lambda *_: (0,) * nd, pipeline_mode=pl.Buffered(1))


def _rms(x, g):
    ms = jnp.mean(x * x, axis=-1, keepdims=True)
    return x * lax.rsqrt(ms + RMS_EPS) * g


def _silu(x):
    return x * jax.nn.sigmoid(x)


def _softplus(x):
    return jnp.maximum(x, 0.0) + jnp.log1p(jnp.exp(-jnp.abs(x)))


def _dot(a, b):
    return jnp.dot(a.astype(BF16), b.astype(BF16), preferred_element_type=F32)


def _dot_nt(a, b):
    return lax.dot_general(a.astype(BF16), b.astype(BF16), (((1,), (1,)), ((), ())),
                           preferred_element_type=F32)


def _dot_tn(a, b):
    return lax.dot_general(a.astype(BF16), b.astype(BF16), (((0,), (0,)), ((), ())),
                           preferred_element_type=F32)


def _norm_proj_kernel(n_out, x_ref, g_ref, *refs):
    w_refs, o_refs = refs[:n_out], refs[n_out:]
    h = _rms(x_ref[...], g_ref[...]).astype(BF16)
    for w_ref, o_ref in zip(w_refs, o_refs):
        n = w_ref.shape[1]
        for c0 in range(0, n, 512):
            c1 = min(c0 + 512, n)
            o_ref[:, c0:c1] = jnp.dot(h, w_ref[:, c0:c1], preferred_element_type=F32)


def _norm_proj(x, g, ws, tm):
    m, d = x.shape
    n_out = len(ws)
    return pl.pallas_call(
        functools.partial(_norm_proj_kernel, n_out),
        grid=(m // tm,),
        in_specs=[pl.BlockSpec((tm, d), lambda i: (i, 0)), _resident((1, d))]
        + [_resident(w.shape) for w in ws],
        out_specs=[pl.BlockSpec((tm, w.shape[1]), lambda i: (i, 0)) for w in ws],
        out_shape=[jax.ShapeDtypeStruct((m, w.shape[1]), F32) for w in ws],
        compiler_params=_cparams(("parallel",)),
        name="norm_proj",
    )(x, g, *ws)


def _even_in_proj_kernel(x_ref, g_ref, wq_ref, wk_ref, wv_ref, wu_ref,
                         q_ref, k_ref, v_ref, u_ref, kt_ref, vt_ref):
    h = _rms(x_ref[...], g_ref[...]).astype(BF16)
    q_ref[...] = jnp.dot(h, wq_ref[...], preferred_element_type=F32)
    u_ref[...] = jnp.dot(h, wu_ref[...], preferred_element_type=F32)
    k = jnp.dot(h, wk_ref[...], preferred_element_type=F32)
    k_ref[...] = k
    kt_ref[...] = k.T
    v = jnp.dot(h, wv_ref[...], preferred_element_type=F32)
    v_ref[...] = v
    vt_ref[...] = v.T


def _even_in_proj(x, g, ws, batch, seq, tm):
    m, d = x.shape
    per_seq = seq // tm
    row = pl.BlockSpec((tm, A_WIDTH), lambda i: (i, 0))
    tr = pl.BlockSpec((None, A_WIDTH, tm), lambda i: (i // per_seq, 0, i % per_seq))
    flat = jax.ShapeDtypeStruct((m, A_WIDTH), F32)
    trs = jax.ShapeDtypeStruct((batch, A_WIDTH, seq), F32)
    return pl.pallas_call(
        _even_in_proj_kernel,
        grid=(m // tm,),
        in_specs=[pl.BlockSpec((tm, d), lambda i: (i, 0)), _resident((1, d))] + [_resident(w.shape) for w in ws],
        out_specs=[row, row, row, row, tr, tr],
        out_shape=[flat, flat, flat, flat, trs, trs],
        compiler_params=_cparams(("parallel",)),
        name="even_in_proj",
    )(x, g, *ws)


def _mix_ffn_kernel(n_mix, final, x_ref, *refs):
    a_refs = refs[:n_mix]
    wo_refs = refs[n_mix:2 * n_mix]
    g_ref, wg_ref, wu_ref, wd_ref = refs[2 * n_mix:2 * n_mix + 4]
    rest = refs[2 * n_mix + 4:]
    if final:
        gf_ref, o_ref, act_ref = rest
    else:
        o_ref, act_ref = rest
    x1 = x_ref[...]
    for a_ref, wo_ref in zip(a_refs, wo_refs):
        x1 = x1 + jnp.dot(a_ref[...].astype(BF16), wo_ref[...], preferred_element_type=F32)
    h = _rms(x1, g_ref[...]).astype(BF16)
    for c0 in range(0, FFN_HIDDEN, FFN_CHUNK):
        gate = jnp.dot(h, wg_ref[:, c0:c0 + FFN_CHUNK], preferred_element_type=F32)
        up = jnp.dot(h, wu_ref[:, c0:c0 + FFN_CHUNK], preferred_element_type=F32)
        act_ref[:, c0:c0 + FFN_CHUNK] = (_silu(gate) * up).astype(BF16)
    out = x1 + jnp.dot(act_ref[...], wd_ref[...], preferred_element_type=F32)
    if final:
        out = _rms(out, gf_ref[...])
    o_ref[...] = out


def _mix_ffn(x, mixes, wos, g, wg, wu, wd, g_final, tm):
    m, d = x.shape
    n_mix = len(mixes)
    final = g_final is not None
    in_specs = [pl.BlockSpec((tm, d), lambda i: (i, 0))]
    in_specs += [pl.BlockSpec((tm, a.shape[1]), lambda i: (i, 0)) for a in mixes]
    in_specs += [_resident(w.shape) for w in wos]
    in_specs += [_resident(g.shape), _resident(wg.shape), _resident(wu.shape), _resident(wd.shape)]
    args = [x, *mixes, *wos, g, wg, wu, wd]
    if final:
        in_specs.append(_resident(g_final.shape))
        args.append(g_final)
    return pl.pallas_call(
        functools.partial(_mix_ffn_kernel, n_mix, final),
        grid=(m // tm,),
        in_specs=in_specs,
        out_specs=pl.BlockSpec((tm, d), lambda i: (i, 0)),
        out_shape=jax.ShapeDtypeStruct((m, d), F32),
        scratch_shapes=[pltpu.VMEM((tm, FFN_HIDDEN), BF16)],
        compiler_params=_cparams(("parallel",)),
        name="mix_ffn",
    )(*args)


def _alibi_slope(h):
    return 2.0 ** (-8.0 * (h + 1.0) / A_HEADS)


def _attn_bias():
    rel = jnp.arange(A_BLOCK)[:, None] + A_BLOCK - jnp.arange(2 * A_BLOCK)[None, :]
    band = (rel >= 0) & (rel <= A_KEYS)
    slopes = jnp.exp2(-8.0 * (jnp.arange(A_HEADS, dtype=F32) + 1.0) / A_HEADS)
    dils = jnp.asarray(A_DILS, F32)
    bias = -slopes[None, :, None, None] * (rel.astype(F32)[None, None] * dils[:, None, None, None])
    bias = jnp.where(band[None, None], bias, NEG)
    bias = bias.reshape(len(A_DILS), N_PAIRS, 2, A_BLOCK, 2 * A_BLOCK)
    return bias.transpose(1, 0, 2, 3, 4)


def _attn_scores(items, bias_ref, d_idx):
    lane = lax.broadcasted_iota(jnp.int32, (A_BLOCK, LANE), 1)
    lo = lane < HEAD_DIM
    each = [(i, h) for i in range(len(items)) for h in range(2)]
    kbs = [kk.astype(BF16) for _, kk, _, _ in items]
    vbs = [vv.astype(BF16) for _, _, vv, _ in items]
    qms = {(i, h): jnp.where(lo if h == 0 else jnp.logical_not(lo), items[i][0] * A_SCALE, 0.0).astype(BF16)
           for i, h in each}
    ss = {(i, h): _dot_nt(qms[i, h], kbs[i]) for i, h in each}
    ss = {(i, h): ss[i, h] + (bias_ref[d_idx, h, :, A_BLOCK:] if items[i][3] else bias_ref[d_idx, h])
          for i, h in each}
    ms = {ih: jnp.max(ss[ih], axis=-1, keepdims=True) for ih in each}
    ps = {ih: jnp.exp(ss[ih] - ms[ih]) for ih in each}
    ls = {ih: jnp.sum(ps[ih], axis=-1, keepdims=True) for ih in each}
    pvs = {(i, h): jnp.dot(ps[i, h].astype(BF16), vbs[i], preferred_element_type=F32) for i, h in each}
    return [tuple(jnp.where(lo, x[i, 0], x[i, 1]) for x in (ms, ls, pvs)) for i in range(len(items))]


def _even_prompt_kernel(seq, q_ref, k_ref, v_ref, u_ref, bias_ref, wp_ref, sc_ref,
                        attn_ref, pool_ref, m_scr, l_scr, acc_scr, pad_scr):
    def rows(dil, start):
        if dil == 1:
            return pl.ds(start, A_BLOCK)
        return pl.ds(start, A_BLOCK, stride=dil)

    def load(dil, start, first):
        cur = rows(dil, start)
        q = q_ref[cur, :]
        if first:
            kk, vv = k_ref[cur, :], v_ref[cur, :]
        elif dil == 1:
            win = pl.ds(start - A_BLOCK, 2 * A_BLOCK)
            kk, vv = k_ref[win, :], v_ref[win, :]
        else:
            prev = rows(dil, start - A_BLOCK * dil)
            kk = jnp.concatenate([k_ref[prev, :], k_ref[cur, :]], axis=0)
            vv = jnp.concatenate([v_ref[prev, :], v_ref[cur, :]], axis=0)
        return cur, (q, kk, vv, first)

    def blocks(d_idx, dil, todo):
        loaded = [load(dil, start, first) for start, first in todo]
        stats = _attn_scores([item for _, item in loaded], bias_ref, d_idx)
        if d_idx != first_branch:
            old = [(m_scr[cur, :], l_scr[cur, :], acc_scr[cur, :]) for cur, _ in loaded]
            merged = []
            for (m, l, acc), (m0, l0, a0) in zip(stats, old):
                mx = jnp.maximum(m0, m)
                w0 = jnp.exp(m0 - mx)
                w1 = jnp.exp(m - mx)
                merged.append((mx, w0 * l0 + w1 * l, w0 * a0 + w1 * acc))
            stats = merged
        for (cur, _), (m, l, acc) in zip(loaded, stats):
            m_scr[cur, :] = m
            l_scr[cur, :] = l
            acc_scr[cur, :] = acc

    first_branch = len(A_DILS) - 1
    for d_idx, dil in reversed(list(enumerate(A_DILS))):
        span = A_BLOCK * dil
        n_blocks = seq // span
        if dil == 1:
            blocks(d_idx, dil, [(0, True)])
            group = 3

            def later(g, c, d_idx=d_idx, dil=dil, span=span, group=group):
                blocks(d_idx, dil, [((1 + g * group + j) * span, False) for j in range(group)])
                return c

            lax.fori_loop(0, (n_blocks - 1) // group, later, 0)
        elif n_blocks > 1:
            def residue(r, c, d_idx=d_idx, dil=dil, span=span, n_blocks=n_blocks):
                blocks(d_idx, dil, [(r, True)] + [(n * span + r, False) for n in range(1, n_blocks)])
                return c

            lax.fori_loop(0, dil, residue, 0)
        else:
            def residues(g, c, d_idx=d_idx, dil=dil):
                blocks(d_idx, dil, [(g * A_INFLIGHT + j, True) for j in range(A_INFLIGHT)])
                return c

            lax.fori_loop(0, dil // A_INFLIGHT, residues, 0)

    attn_ref[...] = acc_scr[...] / l_scr[...]

    p_idx = pl.program_id(1)
    u = u_ref[...]
    pad_scr[0:POOL_PAD, :] = jnp.zeros((POOL_PAD, LANE), F32)
    s = u
    shift = 1
    for level in range(len(POOL_WINDOWS)):
        pad_scr[POOL_PAD:POOL_PAD + seq, :] = s
        nxt = s + pad_scr[POOL_PAD - shift:POOL_PAD - shift + seq, :]
        s = nxt if level == 0 else jnp.where(p_idx >= level, nxt, s)
        shift *= 2
    pos = lax.broadcasted_iota(jnp.int32, (seq, 1), 0)
    window = lax.shift_left(jnp.int32(2), p_idx)
    cnt = jnp.minimum(pos + 1, window).astype(F32)
    pooled = s / cnt - u
    pool_ref[...] = jnp.dot(pooled.astype(BF16), wp_ref[...], preferred_element_type=F32) * sc_ref[...]


def _even_prompt(q, k, v, u, bias, w_pool, scale, batch, seq):
    m = batch * seq
    assert (seq // A_BLOCK - 1) % 3 == 0 and seq % (A_BLOCK * max(A_DILS)) == 0
    assert all(dil % A_INFLIGHT == 0 for dil in A_DILS if seq == A_BLOCK * dil)
    col = lambda b, p: (b, p)
    blk = pl.BlockSpec((seq, LANE), col)
    return pl.pallas_call(
        functools.partial(_even_prompt_kernel, seq),
        grid=(batch, N_PAIRS),
        in_specs=[blk, blk, blk, blk,
                  pl.BlockSpec((None,) + bias.shape[1:], lambda b, p: (p, 0, 0, 0, 0)),
                  pl.BlockSpec((None, POOL_GROUP_DIM, POOL_GROUP_DIM), lambda b, p: (p, 0, 0)),
                  pl.BlockSpec((None, 1, POOL_GROUP_DIM), lambda b, p: (p, 0, 0))],
        out_specs=[blk, blk],
        out_shape=[jax.ShapeDtypeStruct((m, A_WIDTH), F32), jax.ShapeDtypeStruct((m, POOL_WIDTH), F32)],
        scratch_shapes=[pltpu.VMEM((seq, LANE), F32), pltpu.VMEM((seq, LANE), F32),
                        pltpu.VMEM((seq, LANE), F32), pltpu.VMEM((POOL_PAD + seq, LANE), F32)],
        compiler_params=_cparams(("parallel", "arbitrary")),
        name="even_prompt",
    )(q, k, v, u, bias, w_pool, scale)


def _kv_decode_kernel(aliased, kt_ref, vt_ref, q_ref, kn_ref, vn_ref, *refs):
    ko_ref, vo_ref, o_ref = refs[2:] if aliased else refs
    hist = kt_ref.shape[1]
    pos = lax.broadcasted_iota(jnp.int32, (1, hist), 1)
    dist = hist - pos
    mult = jnp.zeros((1, hist), F32)
    for window, dil in A_BRANCHES:
        member = jnp.logical_and(dist <= window, jnp.bitwise_and(dist, dil - 1) == 0)
        mult = mult + member.astype(F32)
    valid = mult > 0.0
    dist_f = dist.astype(F32)
    n_branches = float(len(A_BRANCHES))
    for h in range(A_HEADS):
        col = slice(h, h + 1)
        slabs = [(slice(d, d + KV_SLAB), slice(h * HEAD_DIM + d, h * HEAD_DIM + d + KV_SLAB))
                 for d in range(0, HEAD_DIM, KV_SLAB)]
        qh = q_ref[:, col] * A_SCALE
        part = None
        for dsl, rsl in slabs:
            t = kt_ref[rsl, :] * qh[dsl]
            part = t if part is None else part + t
        s = jnp.sum(part, axis=0, keepdims=True) - _alibi_slope(h) * dist_f
        s = jnp.where(valid, s, NEG)
        s_new = jnp.sum(kn_ref[:, col] * qh, axis=0, keepdims=True)
        m = jnp.maximum(jnp.max(s, axis=-1, keepdims=True), s_new)
        p = mult * jnp.exp(s - m)
        p_new = n_branches * jnp.exp(s_new - m)
        inv_l = 1.0 / (jnp.sum(p, axis=-1, keepdims=True) + p_new)
        for dsl, rsl in slabs:
            pv = jnp.sum(vt_ref[rsl, :] * p, axis=-1, keepdims=True)
            o_ref[dsl, col] = (pv + vn_ref[dsl, col] * p_new) * inv_l

    last = slice(hist - LANE, hist)
    newest = lax.broadcasted_iota(jnp.int32, (KV_SLAB, LANE), 1) == LANE - 1
    for src_ref, new_ref, dst_ref in ((kt_ref, kn_ref, ko_ref), (vt_ref, vn_ref, vo_ref)):
        for h in range(A_HEADS):
            for d in range(0, HEAD_DIM, KV_SLAB):
                rsl = slice(h * HEAD_DIM + d, h * HEAD_DIM + d + KV_SLAB)
                rolled = pltpu.roll(src_ref[rsl, :], hist - 1, axis=1)
                dst_ref[rsl, 0:hist - LANE] = rolled[:, 0:hist - LANE]
                dst_ref[rsl, last] = jnp.where(newest, new_ref[d:d + KV_SLAB, h:h + 1], rolled[:, last])


def _kv_decode(layer, kt, vt, q_col, kn_col, vn_col, prev):
    _, nb, width, hist = kt.shape
    blk = pl.BlockSpec((None, None, width, hist), lambda b: (layer, b, 0, 0))
    col = pl.BlockSpec((None, HEAD_DIM, A_HEADS), lambda b: (b, 0, 0))
    in_specs = [blk, blk, col, col, col]
    args = [kt, vt, q_col, kn_col, vn_col]
    aliases = {}
    if prev is not None:
        in_specs += [pl.BlockSpec(memory_space=pl.ANY)] * 2
        aliases = {len(args): 0, len(args) + 1: 1}
        args += list(prev)
    return pl.pallas_call(
        functools.partial(_kv_decode_kernel, prev is not None),
        grid=(nb,),
        in_specs=in_specs,
        out_specs=[blk, blk, col],
        out_shape=[jax.ShapeDtypeStruct(kt.shape, F32), jax.ShapeDtypeStruct(vt.shape, F32),
                   jax.ShapeDtypeStruct((nb, HEAD_DIM, A_HEADS), F32)],
        input_output_aliases=aliases,
        compiler_params=_cparams(("arbitrary",)),
        name="kv_decode",
    )(*args)


def _pool_decode_kernel(u_ref, hist_ref, wp_ref, sc_ref, o_ref):
    u = u_ref[...]
    for g, w in enumerate(POOL_WINDOWS):
        cols = slice(g * POOL_GROUP_DIM, (g + 1) * POOL_GROUP_DIM)
        cur = u[:, cols]
        tot = cur
        for j in range(POOL_HIST - (w - 1), POOL_HIST):
            tot = tot + hist_ref[j, :, cols]
        pooled = tot / float(w) - cur
        o_ref[:, cols] = jnp.dot(pooled.astype(BF16), wp_ref[g], preferred_element_type=F32) * sc_ref[:, cols]


def _pool_decode(u, hist_t, w_pool, scale):
    nb = u.shape[0]
    return pl.pallas_call(
        _pool_decode_kernel,
        out_shape=jax.ShapeDtypeStruct((nb, POOL_WIDTH), F32),
        compiler_params=pltpu.CompilerParams(vmem_limit_bytes=VMEM_LIMIT),
        name="pool_decode",
    )(u, hist_t, w_pool, scale)


def _gates(ba, alog_ref, dtb_ref):
    beta = jax.nn.sigmoid(ba)
    g = -jnp.exp(alog_ref[...]) * _softplus(ba + dtb_ref[...])
    return beta, g


def _l2norm(x):
    return x * lax.rsqrt(jnp.sum(x * x, axis=-1, keepdims=True) + 1e-6)


def _gated_out(o, gate, nw):
    o = o * lax.rsqrt(jnp.mean(o * o, axis=-1, keepdims=True) + RMS_EPS) * nw
    return o * _silu(gate)


def _gdn_prompt_kernel(qkv_ref, halo_ref, gate_ref, ba_ref, wc_ref, alog_ref, dtb_ref, nw_ref,
                       o_ref, sfin_ref, s_scr, xx_scr):
    step = pl.program_id(1)
    rows = GDN_ROWS

    @pl.when(step == 0)
    def _():
        s_scr[...] = jnp.zeros(s_scr.shape, F32)

    xx_scr[0:HALO, :] = jnp.where(step == 0, 0.0, halo_ref[...])
    xx_scr[HALO:HALO + rows, :] = qkv_ref[...]

    def conv(c0):
        cols = slice(c0, c0 + GDN_DK)
        acc = wc_ref[CONV_WIDTH - 1:CONV_WIDTH, cols] * xx_scr[HALO:HALO + rows, cols]
        for i in range(1, CONV_WIDTH):
            acc = acc + wc_ref[CONV_WIDTH - 1 - i:CONV_WIDTH - i, cols] * xx_scr[HALO - i:HALO - i + rows, cols]
        return _silu(acc)

    beta_all, g_all = _gates(ba_ref[...], alog_ref, dtb_ref)
    row = lax.broadcasted_iota(jnp.int32, (rows, LANE), 0)
    in_chunk = jnp.bitwise_and(row, GDN_CHUNK - 1)
    gcs = g_all
    shift = 1
    while shift < GDN_CHUNK:
        gcs = gcs + jnp.where(in_chunk >= shift, pltpu.roll(gcs, shift, axis=0), 0.0)
        shift *= 2
    glast = jnp.concatenate(
        [jnp.broadcast_to(gcs[(c + 1) * GDN_CHUNK - 1:(c + 1) * GDN_CHUNK, :], (GDN_CHUNK, LANE))
         for c in range(GDN_NC)], axis=0)
    eg_all = jnp.exp(gcs)
    ekd_all = jnp.exp(glast - gcs)
    gcs_t = gcs.T
    eglast_t = jnp.exp(gcs_t)

    ri = lax.broadcasted_iota(jnp.int32, (rows, rows), 0)
    ci = lax.broadcasted_iota(jnp.int32, (rows, rows), 1)
    same = lax.shift_right_logical(ri, GDN_CHUNK_SHIFT) == lax.shift_right_logical(ci, GDN_CHUNK_SHIFT)
    bd_tril = jnp.logical_and(same, ri >= ci)
    bd_strict = jnp.logical_and(same, ri > ci)
    eye_cat = (lax.broadcasted_iota(jnp.int32, (GDN_CHUNK, rows), 0)
               == jnp.bitwise_and(lax.broadcasted_iota(jnp.int32, (GDN_CHUNK, rows), 1), GDN_CHUNK - 1)
               ).astype(F32)

    def block_of(x, size):
        return lax.shift_right_logical(x, size.bit_length() - 1)

    m_same = same.astype(BF16)
    m_base = (block_of(ri, GDN_BASE) == block_of(ci, GDN_BASE)).astype(BF16)
    m_merge = []
    size = GDN_BASE
    while size < GDN_CHUNK:
        pair = block_of(ri, 2 * size) == block_of(ci, 2 * size)
        lower_left = jnp.logical_and(jnp.bitwise_and(block_of(ri, size), 1) == 1,
                                     jnp.bitwise_and(block_of(ci, size), 1) == 0)
        m_merge.append(jnp.logical_and(pair, lower_left).astype(BF16))
        size *= 2

    def to_cat(x):
        out = x[0:GDN_CHUNK]
        for c in range(1, GDN_NC):
            out = out + x[c * GDN_CHUNK:(c + 1) * GDN_CHUNK]
        return out

    def to_bd(x):
        return jnp.concatenate([x.astype(BF16)] * GDN_NC, axis=0) * m_same

    def mm(a, b):
        return jnp.dot(a.astype(BF16), b, preferred_element_type=F32)

    def local(h):
        gi = GDN_HEADS + h
        beta = beta_all[:, h:h + 1]
        egc = eg_all[:, gi:gi + 1]
        q = _l2norm(conv(h * GDN_DK)) * (GDN_DK ** -0.5)
        k = _l2norm(conv(GDN_QK + h * GDN_DK))
        v = conv(2 * GDN_QK + h * GDN_DK)
        kb = k * beta
        prod = _dot_nt(jnp.concatenate([kb, q], axis=0), k)
        diff = gcs[:, gi:gi + 1] - gcs_t[gi:gi + 1, :]
        decay = jnp.where(bd_tril, jnp.exp(jnp.where(bd_tril, diff, 0.0)), 0.0)
        return dict(neg_l=jnp.where(bd_strict, -(prod[:rows] * decay), 0.0),
                    a_intra=(prod[rows:] * decay).astype(BF16),
                    rhs=jnp.concatenate([v * beta, kb * egc], axis=1).astype(BF16),
                    qdec=(q * egc).astype(BF16),
                    kdec=(k * ekd_all[:, gi:gi + 1]).astype(BF16))

    for g0 in range(0, GDN_HEADS, GDN_GROUP):
        heads = list(range(g0, g0 + GDN_GROUP))
        loc = [local(h) for h in heads]
        neg_l = [x["neg_l"].astype(BF16) for x in loc]
        p_bd = [n * m_base for n in neg_l]
        p_cat = [to_cat(p) for p in p_bd]
        inv = [eye_cat + p.astype(F32) for p in p_cat]
        p_cat = [mm(p, b) for p, b in zip(p_cat, p_bd)]
        power = 2
        while 2 * power < GDN_BASE:
            res = [mm(jnp.concatenate([p, t], axis=0), to_bd(p)) for p, t in zip(p_cat, inv)]
            p_cat = [r[:GDN_CHUNK] for r in res]
            inv = [t + r[GDN_CHUNK:] for t, r in zip(inv, res)]
            power *= 2
        inv = [t + mm(t, to_bd(p)) for p, t in zip(p_cat, inv)]
        for m_off in m_merge:
            half = [mm(t, n * m_off) for t, n in zip(inv, neg_l)]
            inv = [t + mm(y, to_bd(t)) for t, y in zip(inv, half)]
        uw = [jnp.dot(to_bd(t), x["rhs"], preferred_element_type=F32) for t, x in zip(inv, loc)]
        state = [s_scr[h] for h in heads]
        v_news = [[] for _ in heads]
        q_states = [[] for _ in heads]
        for c in range(GDN_NC):
            r = slice(c * GDN_CHUNK, (c + 1) * GDN_CHUNK)
            last = (c + 1) * GDN_CHUNK - 1
            for j, h in enumerate(heads):
                gi = GDN_HEADS + h
                lhs = jnp.concatenate([uw[j][r, GDN_DK:].astype(BF16), loc[j]["qdec"][r]], axis=0)
                res = jnp.dot(lhs, state[j].astype(BF16), preferred_element_type=F32)
                v_new = uw[j][r, :GDN_DK] - res[:GDN_CHUNK]
                q_states[j].append(res[GDN_CHUNK:])
                v_news[j].append(v_new)
                state[j] = state[j] * eglast_t[gi:gi + 1, last:last + 1] + _dot_tn(loc[j]["kdec"][r], v_new)
        for j, h in enumerate(heads):
            s_scr[h] = state[j]
            o = jnp.concatenate(q_states[j], axis=0) + jnp.dot(
                loc[j]["a_intra"], jnp.concatenate(v_news[j], axis=0).astype(BF16), preferred_element_type=F32)
            cols = slice(h * GDN_DK, (h + 1) * GDN_DK)
            o_ref[:, cols] = _gated_out(o, gate_ref[:, cols], nw_ref[...])

    @pl.when(step == pl.num_programs(1) - 1)
    def _():
        sfin_ref[...] = s_scr[...]


def _gdn_prompt(qkv, gate, ba, w_conv, alog, dtb, nw, batch, seq):
    m = batch * seq
    steps = seq // GDN_ROWS
    per_halo = GDN_ROWS // HALO
    row_blk = lambda b, c: (b * steps + c, 0)
    return pl.pallas_call(
        _gdn_prompt_kernel,
        grid=(batch, steps),
        in_specs=[pl.BlockSpec((GDN_ROWS, GDN_CONV_CH), row_blk),
                  pl.BlockSpec((HALO, GDN_CONV_CH),
                               lambda b, c: (jnp.maximum((b * steps + c) * per_halo - 1, 0), 0)),
                  pl.BlockSpec((GDN_ROWS, GDN_VW), row_blk),
                  pl.BlockSpec((GDN_ROWS, LANE), row_blk),
                  _resident(w_conv.shape), _resident(alog.shape), _resident(dtb.shape), _resident(nw.shape)],
        out_specs=[pl.BlockSpec((GDN_ROWS, GDN_VW), row_blk),
                   pl.BlockSpec((None, GDN_HEADS, GDN_DK, GDN_DK), lambda b, c: (b, 0, 0, 0))],
        out_shape=[jax.ShapeDtypeStruct((m, GDN_VW), F32),
                   jax.ShapeDtypeStruct((batch, GDN_HEADS, GDN_DK, GDN_DK), F32)],
        scratch_shapes=[pltpu.VMEM((GDN_HEADS, GDN_DK, GDN_DK), F32),
                        pltpu.VMEM((HALO + GDN_ROWS, GDN_CONV_CH), F32)],
        compiler_params=_cparams(("parallel", "arbitrary")),
        name="gdn_prompt",
    )(qkv, qkv, gate, ba, w_conv, alog, dtb, nw)


def _gdn_decode_prep_kernel(x_ref, hist_ref, ba_ref, wc_ref, alog_ref, dtb_ref, q_ref, k_ref, v_ref, bg_ref):
    for h in range(3 * GDN_HEADS):
        cols = slice(h * GDN_DK, (h + 1) * GDN_DK)
        acc = wc_ref[CONV_WIDTH - 1:CONV_WIDTH, cols] * x_ref[:, cols]
        for i in range(1, CONV_WIDTH):
            acc = acc + wc_ref[CONV_WIDTH - 1 - i:CONV_WIDTH - i, cols] * hist_ref[CONV_WIDTH - 1 - i, :, cols]
        y = _silu(acc)
        if h < GDN_HEADS:
            q_ref[:, cols] = _l2norm(y) * (GDN_DK ** -0.5)
        elif h < 2 * GDN_HEADS:
            k_ref[:, slice((h - GDN_HEADS) * GDN_DK, (h - GDN_HEADS + 1) * GDN_DK)] = _l2norm(y)
        else:
            v_ref[:, slice((h - 2 * GDN_HEADS) * GDN_DK, (h - 2 * GDN_HEADS + 1) * GDN_DK)] = y
    beta, g = _gates(ba_ref[...], alog_ref, dtb_ref)
    lane = lax.broadcasted_iota(jnp.int32, beta.shape, 1)
    bg_ref[...] = jnp.where(lane < GDN_HEADS, beta, g)


def _gdn_decode_prep(x, hist_t, ba, w_conv, alog, dtb):
    nb = x.shape[0]
    head = jax.ShapeDtypeStruct((nb, GDN_QK), F32)
    return pl.pallas_call(
        _gdn_decode_prep_kernel,
        out_shape=[head, head, head, jax.ShapeDtypeStruct((nb, LANE), F32)],
        compiler_params=pltpu.CompilerParams(vmem_limit_bytes=VMEM_LIMIT),
        name="gdn_decode_prep",
    )(x, hist_t, ba, w_conv, alog, dtb)


def _gdn_decode_kernel(q_ref, k_ref, v_ref, gate_ref, beta_ref, g_ref, s_ref, nw_ref, o_ref, so_ref):
    pad = jnp.zeros((6, GDN_DK), F32)
    for t in range(q_ref.shape[0]):
        q_all, k_all, v_all = q_ref[t], k_ref[t], v_ref[t]
        beta_all = beta_ref[t]
        eg_all = jnp.exp(g_ref[t])
        qk_all = jnp.sum(q_all * k_all, axis=-1, keepdims=True)
        outs = []
        for h in range(GDN_HEADS):
            s = s_ref[t, h]
            q, k, v = q_all[h:h + 1], k_all[h:h + 1], v_all[h:h + 1]
            beta, eg = beta_all[h:h + 1], eg_all[h:h + 1]
            res = _dot(jnp.concatenate([k, q, pad], axis=0), s)
            v_new = beta * (v - eg * res[0:1])
            o = eg * res[1:2] + qk_all[h:h + 1] * v_new
            k_col = jnp.broadcast_to(k, (GDN_DK, GDN_DK)).T
            so_ref[t, h] = s * eg + k_col * v_new
            outs.append(o)
        o_ref[t] = _gated_out(jnp.concatenate(outs, axis=0), gate_ref[t], nw_ref[...])


def _gdn_decode(layer, q3, k3, v3, gate3, beta3, g3, state, nw):
    nb = q3.shape[0]
    tb = GDN_DECODE_TOKENS
    tok = pl.BlockSpec((tb, GDN_HEADS, GDN_DK), lambda b: (b, 0, 0))
    sc = pl.BlockSpec((tb, GDN_HEADS, 1), lambda b: (b, 0, 0))
    return pl.pallas_call(
        _gdn_decode_kernel,
        grid=(nb // tb,),
        in_specs=[tok, tok, tok, tok, sc, sc,
                  pl.BlockSpec((None, tb, GDN_HEADS, GDN_DK, GDN_DK), lambda b: (layer, b, 0, 0, 0)),
                  _resident(nw.shape)],
        out_specs=[tok, pl.BlockSpec((tb, GDN_HEADS, GDN_DK, GDN_DK), lambda b: (b, 0, 0, 0))],
        out_shape=[jax.ShapeDtypeStruct((nb, GDN_HEADS, GDN_DK), F32),
                   jax.ShapeDtypeStruct((nb, GDN_HEADS, GDN_DK, GDN_DK), F32)],
        compiler_params=_cparams(("parallel",)),
        name="gdn_decode",
    )(q3, k3, v3, gate3, beta3, g3, state, nw)


def _pad_cols(x, start, width=LANE):
    return jnp.pad(x, ((0, 0), (start, width - start - x.shape[1])))


def _positions_minor(x):
    lead = x.shape[:-3]
    n = len(lead)
    xt = x.transpose(*range(n), n + 1, n + 2, n)
    return xt.reshape(*lead, x.shape[-2] * x.shape[-1], x.shape[-3])


def _positions_major(xt, heads):
    lead = xt.shape[:-2]
    n = len(lead)
    x = xt.reshape(*lead, heads, xt.shape[-2] // heads, xt.shape[-1])
    return x.transpose(*range(n), n + 2, n, n + 1)


def kernel(x_prompt, x_sample, state_a_k, state_a_v, state_pool, state_conv, state_gdn, norm_mix, norm_ffn,
           w_in_a, w_out_a, w_pool, pool_scale, w_in_c, w_conv_c, a_log_c, dt_bias_c, norm_out_c, w_out_c,
           w_ffn_gate, w_ffn_up, w_ffn_down, norm_final):
    batch, seq, d = x_prompt.shape
    nb = x_sample.shape[0]
    depth = norm_mix.shape[0]
    tm = 512

    w_in_a_b = w_in_a.astype(BF16)
    ws_even = [[w_in_a_b[i][:, j * A_WIDTH:(j + 1) * A_WIDTH] for j in range(4)] for i in range(w_in_a.shape[0])]
    w_out_a_b = w_out_a.astype(BF16)
    w_pool_b = w_pool.astype(BF16)
    w_in_c_b = w_in_c.astype(BF16)
    ws_odd = [[w_in_c_b[i][:, :GDN_CONV_CH], w_in_c_b[i][:, GDN_CONV_CH:GDN_CONV_CH + GDN_VW],
               _pad_cols(w_in_c_b[i][:, GDN_CONV_CH + GDN_VW:], 0)] for i in range(w_in_c.shape[0])]
    w_out_c_b = w_out_c.astype(BF16)
    wg_b, wu_b, wd_b = w_ffn_gate.astype(BF16), w_ffn_up.astype(BF16), w_ffn_down.astype(BF16)
    alog_p = _pad_cols(a_log_c, GDN_HEADS)
    dtb_p = _pad_cols(dt_bias_c, GDN_HEADS)
    bias = _attn_bias()
    g_final = norm_final.reshape(1, d)

    def ffn_args(layer):
        last = layer == depth - 1
        return (norm_ffn[layer:layer + 1], wg_b[layer], wu_b[layer], wd_b[layer], g_final if last else None)

    xs = x_sample.reshape(nb, d)
    kt_hist, vt_hist = _positions_minor(state_a_k), _positions_minor(state_a_v)
    kv_out = None
    pool_s, conv_s, gdn_s = [], [], []
    for layer in range(depth):
        i = layer // 2
        g_mix = norm_mix[layer:layer + 1]
        if layer % 2 == 0:
            q, k, v, u = _norm_proj(xs, g_mix, ws_even[i], nb)
            as_col = lambda a: a.reshape(nb, A_HEADS, HEAD_DIM).transpose(0, 2, 1)
            k_out, v_out, attn = _kv_decode(i, kt_hist, vt_hist, as_col(q), as_col(k), as_col(v), kv_out)
            kv_out = (k_out, v_out)
            pooled = _pool_decode(u, state_pool[i].transpose(1, 0, 2), w_pool_b[i], pool_scale[i:i + 1])
            pool_s.append(jnp.concatenate([state_pool[i][:, 1:], u[:, None, :]], axis=1))
            mixes = [attn.transpose(0, 2, 1).reshape(nb, A_WIDTH), pooled]
            wos = [w_out_a_b[i][:A_WIDTH], w_out_a_b[i][A_WIDTH:]]
        else:
            qkv, gate, ba = _norm_proj(xs, g_mix, ws_odd[i], nb)
            qn, kn, vv, bg = _gdn_decode_prep(qkv, state_conv[i].transpose(1, 0, 2), ba, w_conv_c[i],
                                              alog_p[i:i + 1], dtb_p[i:i + 1])
            shape3 = (nb, GDN_HEADS, GDN_DK)
            o3, s_new = _gdn_decode(i, qn.reshape(shape3), kn.reshape(shape3), vv.reshape(shape3),
                                    gate.reshape(shape3), bg[:, :GDN_HEADS].reshape(nb, GDN_HEADS, 1),
                                    bg[:, GDN_HEADS:2 * GDN_HEADS].reshape(nb, GDN_HEADS, 1),
                                    state_gdn, norm_out_c[i:i + 1])
            conv_s.append(jnp.concatenate([state_conv[i][:, 1:], qkv[:, None, :]], axis=1))
            gdn_s.append(s_new)
            mixes = [o3.reshape(nb, GDN_VW)]
            wos = [w_out_c_b[i]]
        xs = _mix_ffn(xs, mixes, wos, *ffn_args(layer), nb)
    y_sample = xs.reshape(nb, 1, d)
    a_k_sample = _positions_major(kv_out[0], A_HEADS)
    a_v_sample = _positions_major(kv_out[1], A_HEADS)

    xp = x_prompt.reshape(batch * seq, d)
    kt_p, vt_p, pool_p, conv_p, gdn_p = [], [], [], [], []
    for layer in range(depth):
        i = layer // 2
        g_mix = norm_mix[layer:layer + 1]
        if layer % 2 == 0:
            q, k, v, u, kt, vt = _even_in_proj(xp, g_mix, ws_even[i], batch, seq, tm)
            attn, pooled = _even_prompt(q, k, v, u, bias, w_pool_b[i],
                                        pool_scale[i].reshape(len(POOL_WINDOWS), 1, POOL_GROUP_DIM), batch, seq)
            kt_p.append(kt)
            vt_p.append(vt)
            pool_p.append(u.reshape(batch, seq, POOL_WIDTH)[:, seq - POOL_HIST:])
            mixes = [attn, pooled]
            wos = [w_out_a_b[i][:A_WIDTH], w_out_a_b[i][A_WIDTH:]]
        else:
            qkv, gate, ba = _norm_proj(xp, g_mix, ws_odd[i], tm)
            o, s_fin = _gdn_prompt(qkv, gate, ba, w_conv_c[i], alog_p[i:i + 1], dtb_p[i:i + 1],
                                   norm_out_c[i:i + 1], batch, seq)
            conv_p.append(qkv.reshape(batch, seq, GDN_CONV_CH)[:, seq - (CONV_WIDTH - 1):])
            gdn_p.append(s_fin)
            mixes = [o]
            wos = [w_out_c_b[i]]
        xp = _mix_ffn(xp, mixes, wos, *ffn_args(layer), tm)
    y_prompt = xp.reshape(batch, seq, d)
    a_k_prompt = _positions_major(jnp.stack(kt_p), A_HEADS)
    a_v_prompt = _positions_major(jnp.stack(vt_p), A_HEADS)

    return (y_prompt, y_sample, a_k_prompt, a_v_prompt, jnp.stack(pool_p), jnp.stack(conv_p),
            jnp.stack(gdn_p), a_k_sample, a_v_sample, jnp.stack(pool_s), jnp.stack(conv_s), jnp.stack(gdn_s))
```

```python
import functools

import jax
import jax.numpy as jnp
from jax import lax
from jax.experimental import pallas as pl
from jax.experimental.pallas import tpu as pltpu

F32 = jnp.float32
BF16 = jnp.bfloat16

D_MODEL = 1024
RMS_EPS = 1e-6
NEG = -1e30

HEAD_DIM = 64
A_HEADS = 8
A_WIDTH = A_HEADS * HEAD_DIM
A_BRANCHES = ((128, 1), (512, 4), (2048, 16))
A_DILS = tuple(d for _, d in A_BRANCHES)
A_KEYS = 128
A_BLOCK = 128
A_SCALE = HEAD_DIM ** -0.5
N_PAIRS = A_HEADS // 2
A_INFLIGHT = 4
KV_SLAB = 8

POOL_WINDOWS = (2, 4, 8, 16)
POOL_WIDTH = 512
POOL_GROUP_DIM = 128
POOL_HIST = 15
POOL_PAD = 16

GDN_HEADS = 8
GDN_DK = 128
GDN_QK = GDN_HEADS * GDN_DK
GDN_VW = GDN_HEADS * GDN_DK
GDN_CONV_CH = 3 * GDN_QK
CONV_WIDTH = 4
GDN_CHUNK = 64
GDN_CHUNK_SHIFT = GDN_CHUNK.bit_length() - 1
GDN_ROWS = 256
GDN_NC = GDN_ROWS // GDN_CHUNK
GDN_GROUP = 8
GDN_BASE = 8
HALO = 8
GDN_DECODE_TOKENS = 4

FFN_HIDDEN = 2816
FFN_CHUNK = 256
LANE = 128

VMEM_LIMIT = 56 * 1024 * 1024


def _cparams(sem):
    return pltpu.CompilerParams(dimension_semantics=sem, vmem_limit_bytes=VMEM_LIMIT)


def _resident(shape):
    nd = len(shape)
    return pl.BlockSpec(shape, lambda *_: (0,) * nd, pipeline_mode=pl.Buffered(1))


def _resident_part(block, index):
    return pl.BlockSpec(block, lambda *_: index, pipeline_mode=pl.Buffered(1))


def _rms(x, g):
    ms = jnp.mean(x * x, axis=-1, keepdims=True)
    return x * lax.rsqrt(ms + RMS_EPS) * g


def _silu(x):
    return x * jax.nn.sigmoid(x)


def _softplus(x):
    return jnp.maximum(x, 0.0) + jnp.log1p(jnp.exp(-jnp.abs(x)))


def _dot(a, b):
    return jnp.dot(a.astype(BF16), b.astype(BF16), preferred_element_type=F32)


def _dot_nt(a, b):
    return lax.dot_general(a.astype(BF16), b.astype(BF16), (((1,), (1,)), ((), ())),
                           preferred_element_type=F32)


def _dot_tn(a, b):
    return lax.dot_general(a.astype(BF16), b.astype(BF16), (((0,), (0,)), ((), ())),
                           preferred_element_type=F32)


def _norm_proj_kernel(n_out, x_ref, g_ref, *refs):
    w_refs, o_refs = refs[:n_out], refs[n_out:]
    h = _rms(x_ref[...], g_ref[...]).astype(BF16)
    for w_ref, o_ref in zip(w_refs, o_refs):
        n = w_ref.shape[1]
        for c0 in range(0, n, 512):
            c1 = min(c0 + 512, n)
            o_ref[:, c0:c1] = jnp.dot(h, w_ref[:, c0:c1], preferred_element_type=F32)


def _norm_proj(x, g, ws, tm):
    m, d = x.shape
    n_out = len(ws)
    widths = [block[-1] for _, block, _ in ws]
    return pl.pallas_call(
        functools.partial(_norm_proj_kernel, n_out),
        grid=(m // tm,),
        in_specs=[pl.BlockSpec((tm, d), lambda i: (i, 0)), _resident((1, d))]
        + [_resident_part(block, index) for _, block, index in ws],
        out_specs=[pl.BlockSpec((tm, n), lambda i: (i, 0)) for n in widths],
        out_shape=[jax.ShapeDtypeStruct((m, n), F32) for n in widths],
        compiler_params=_cparams(("parallel",)),
        name="norm_proj",
    )(x, g, *[w for w, _, _ in ws])


def _even_in_proj_kernel(aliased, x_ref, g_ref, wq_ref, wk_ref, wv_ref, wu_ref, *refs):
    q_ref, k_ref, v_ref, u_ref, kt_ref, vt_ref = refs[2:] if aliased else refs
    h = _rms(x_ref[...], g_ref[...]).astype(BF16)
    q_ref[...] = jnp.dot(h, wq_ref[...], preferred_element_type=F32)
    u_ref[...] = jnp.dot(h, wu_ref[...], preferred_element_type=F32)
    k = jnp.dot(h, wk_ref[...], preferred_element_type=F32)
    k_ref[...] = k
    kt_ref[...] = k.T
    v = jnp.dot(h, wv_ref[...], preferred_element_type=F32)
    v_ref[...] = v
    vt_ref[...] = v.T


def _even_in_proj(x, g, ws, slot, n_slots, prev, batch, seq, tm):
    m, d = x.shape
    per_seq = seq // tm
    row = pl.BlockSpec((tm, A_WIDTH), lambda i: (i, 0))
    tr = pl.BlockSpec((None, None, A_WIDTH, tm), lambda i: (slot, i // per_seq, 0, i % per_seq))
    flat = jax.ShapeDtypeStruct((m, A_WIDTH), F32)
    trs = jax.ShapeDtypeStruct((n_slots, batch, A_WIDTH, seq), F32)
    in_specs = [pl.BlockSpec((tm, d), lambda i: (i, 0)), _resident((1, d))]
    in_specs += [_resident_part(block, index) for _, block, index in ws]
    args = [x, g] + [w for w, _, _ in ws]
    aliases = {}
    if prev is not None:
        in_specs += [pl.BlockSpec(memory_space=pl.ANY)] * 2
        aliases = {len(args): 4, len(args) + 1: 5}
        args += list(prev)
    return pl.pallas_call(
        functools.partial(_even_in_proj_kernel, prev is not None),
        grid=(m // tm,),
        in_specs=in_specs,
        out_specs=[row, row, row, row, tr, tr],
        out_shape=[flat, flat, flat, flat, trs, trs],
        input_output_aliases=aliases,
        compiler_params=_cparams(("arbitrary",)),
        name="even_in_proj",
    )(*args)


def _mix_ffn_kernel(n_mix, final, x_ref, *refs):
    a_refs = refs[:n_mix]
    wo_refs = refs[n_mix:2 * n_mix]
    g_ref, wg_ref, wu_ref, wd_ref = refs[2 * n_mix:2 * n_mix + 4]
    rest = refs[2 * n_mix + 4:]
    if final:
        gf_ref, o_ref, act_ref = rest
    else:
        o_ref, act_ref = rest
    x1 = x_ref[...]
    for a_ref, wo_ref in zip(a_refs, wo_refs):
        x1 = x1 + jnp.dot(a_ref[...].astype(BF16), wo_ref[...], preferred_element_type=F32)
    h = _rms(x1, g_ref[...]).astype(BF16)
    for c0 in range(0, FFN_HIDDEN, FFN_CHUNK):
        gate = jnp.dot(h, wg_ref[:, c0:c0 + FFN_CHUNK], preferred_element_type=F32)
        up = jnp.dot(h, wu_ref[:, c0:c0 + FFN_CHUNK], preferred_element_type=F32)
        act_ref[:, c0:c0 + FFN_CHUNK] = (_silu(gate) * up).astype(BF16)
    out = x1 + jnp.dot(act_ref[...], wd_ref[...], preferred_element_type=F32)
    if final:
        out = _rms(out, gf_ref[...])
    o_ref[...] = out


def _mix_ffn(x, mixes, wos, g, wg, wu, wd, g_final, tm):
    m, d = x.shape
    n_mix = len(mixes)
    final = g_final is not None
    in_specs = [pl.BlockSpec((tm, d), lambda i: (i, 0))]
    in_specs += [pl.BlockSpec((tm, a.shape[1]), lambda i: (i, 0)) for a in mixes]
    in_specs += [_resident_part(block, index) for _, block, index in wos]
    in_specs += [_resident(g.shape)] + [_resident_part(block, index) for _, block, index in (wg, wu, wd)]
    args = [x, *mixes, *[w for w, _, _ in wos], g, *[w for w, _, _ in (wg, wu, wd)]]
    if final:
        in_specs.append(_resident(g_final.shape))
        args.append(g_final)
    return pl.pallas_call(
        functools.partial(_mix_ffn_kernel, n_mix, final),
        grid=(m // tm,),
        in_specs=in_specs,
        out_specs=pl.BlockSpec((tm, d), lambda i: (i, 0)),
        out_shape=jax.ShapeDtypeStruct((m, d), F32),
        scratch_shapes=[pltpu.VMEM((tm, FFN_HIDDEN), BF16)],
        compiler_params=_cparams(("parallel",)),
        name="mix_ffn",
    )(*args)


def _alibi_slope(h):
    return 2.0 ** (-8.0 * (h + 1.0) / A_HEADS)


def _attn_bias():
    rel = jnp.arange(A_BLOCK)[:, None] + A_BLOCK - jnp.arange(2 * A_BLOCK)[None, :]
    band = (rel >= 0) & (rel <= A_KEYS)
    slopes = jnp.exp2(-8.0 * (jnp.arange(A_HEADS, dtype=F32) + 1.0) / A_HEADS)
    dils = jnp.asarray(A_DILS, F32)
    bias = -slopes[None, :, None, None] * (rel.astype(F32)[None, None] * dils[:, None, None, None])
    bias = jnp.where(band[None, None], bias, NEG)
    bias = bias.reshape(len(A_DILS), N_PAIRS, 2, A_BLOCK, 2 * A_BLOCK)
    return bias.transpose(1, 0, 2, 3, 4)


def _attn_scores(items, bias_ref, d_idx):
    lane = lax.broadcasted_iota(jnp.int32, (A_BLOCK, LANE), 1)
    lo = lane < HEAD_DIM
    each = [(i, h) for i in range(len(items)) for h in range(2)]
    kbs = [kk.astype(BF16) for _, kk, _, _ in items]
    vbs = [vv.astype(BF16) for _, _, vv, _ in items]
    qms = {(i, h): jnp.where(lo if h == 0 else jnp.logical_not(lo), items[i][0] * A_SCALE, 0.0).astype(BF16)
           for i, h in each}
    ss = {(i, h): _dot_nt(qms[i, h], kbs[i]) for i, h in each}
    ss = {(i, h): ss[i, h] + (bias_ref[d_idx, h, :, A_BLOCK:] if items[i][3] else bias_ref[d_idx, h])
          for i, h in each}
    ms = {ih: jnp.max(ss[ih], axis=-1, keepdims=True) for ih in each}
    ps = {ih: jnp.exp(ss[ih] - ms[ih]) for ih in each}
    ls = {ih: jnp.sum(ps[ih], axis=-1, keepdims=True) for ih in each}
    pvs = {(i, h): jnp.dot(ps[i, h].astype(BF16), vbs[i], preferred_element_type=F32) for i, h in each}
    return [tuple(jnp.where(lo, x[i, 0], x[i, 1]) for x in (ms, ls, pvs)) for i in range(len(items))]


def _even_prompt_kernel(seq, q_ref, k_ref, v_ref, u_ref, bias_ref, wp_ref, sc_ref,
                        attn_ref, pool_ref, m_scr, l_scr, acc_scr, pad_scr):
    def rows(dil, start):
        if dil == 1:
            return pl.ds(start, A_BLOCK)
        return pl.ds(start, A_BLOCK, stride=dil)

    def load(dil, start, first):
        cur = rows(dil, start)
        q = q_ref[cur, :]
        if first:
            kk, vv = k_ref[cur, :], v_ref[cur, :]
        elif dil == 1:
            win = pl.ds(start - A_BLOCK, 2 * A_BLOCK)
            kk, vv = k_ref[win, :], v_ref[win, :]
        else:
            prev = rows(dil, start - A_BLOCK * dil)
            kk = jnp.concatenate([k_ref[prev, :], k_ref[cur, :]], axis=0)
            vv = jnp.concatenate([v_ref[prev, :], v_ref[cur, :]], axis=0)
        return cur, (q, kk, vv, first)

    def blocks(d_idx, dil, todo):
        loaded = [load(dil, start, first) for start, first in todo]
        stats = _attn_scores([item for _, item in loaded], bias_ref, d_idx)
        if d_idx != first_branch:
            old = [(m_scr[cur, :], l_scr[cur, :], acc_scr[cur, :]) for cur, _ in loaded]
            merged = []
            for (m, l, acc), (m0, l0, a0) in zip(stats, old):
                mx = jnp.maximum(m0, m)
                w0 = jnp.exp(m0 - mx)
                w1 = jnp.exp(m - mx)
                merged.append((mx, w0 * l0 + w1 * l, w0 * a0 + w1 * acc))
            stats = merged
        for (cur, _), (m, l, acc) in zip(loaded, stats):
            m_scr[cur, :] = m
            l_scr[cur, :] = l
            acc_scr[cur, :] = acc

    first_branch = len(A_DILS) - 1
    for d_idx, dil in reversed(list(enumerate(A_DILS))):
        span = A_BLOCK * dil
        n_blocks = seq // span
        if dil == 1:
            blocks(d_idx, dil, [(0, True)])
            group = 3

            def later(g, c, d_idx=d_idx, dil=dil, span=span, group=group):
                blocks(d_idx, dil, [((1 + g * group + j) * span, False) for j in range(group)])
                return c

            lax.fori_loop(0, (n_blocks - 1) // group, later, 0)
        elif n_blocks > 1:
            def residue(r, c, d_idx=d_idx, dil=dil, span=span, n_blocks=n_blocks):
                blocks(d_idx, dil, [(r, True)] + [(n * span + r, False) for n in range(1, n_blocks)])
                return c

            lax.fori_loop(0, dil, residue, 0)
        else:
            def residues(g, c, d_idx=d_idx, dil=dil):
                blocks(d_idx, dil, [(g * A_INFLIGHT + j, True) for j in range(A_INFLIGHT)])
                return c

            lax.fori_loop(0, dil // A_INFLIGHT, residues, 0)

    attn_ref[...] = acc_scr[...] / l_scr[...]

    p_idx = pl.program_id(1)
    u = u_ref[...]
    pad_scr[0:POOL_PAD, :] = jnp.zeros((POOL_PAD, LANE), F32)
    s = u
    shift = 1
    for level in range(len(POOL_WINDOWS)):
        pad_scr[POOL_PAD:POOL_PAD + seq, :] = s
        nxt = s + pad_scr[POOL_PAD - shift:POOL_PAD - shift + seq, :]
        s = nxt if level == 0 else jnp.where(p_idx >= level, nxt, s)
        shift *= 2
    pos = lax.broadcasted_iota(jnp.int32, (seq, 1), 0)
    window = lax.shift_left(jnp.int32(2), p_idx)
    cnt = jnp.minimum(pos + 1, window).astype(F32)
    pooled = s / cnt - u
    pool_ref[...] = jnp.dot(pooled.astype(BF16), wp_ref[...], preferred_element_type=F32) * sc_ref[...]


def _even_prompt(q, k, v, u, bias, w_pool, scale, batch, seq):
    m = batch * seq
    assert (seq // A_BLOCK - 1) % 3 == 0 and seq % (A_BLOCK * max(A_DILS)) == 0
    assert all(dil % A_INFLIGHT == 0 for dil in A_DILS if seq == A_BLOCK * dil)
    col = lambda b, p: (b, p)
    blk = pl.BlockSpec((seq, LANE), col)
    return pl.pallas_call(
        functools.partial(_even_prompt_kernel, seq),
        grid=(batch, N_PAIRS),
        in_specs=[blk, blk, blk, blk,
                  pl.BlockSpec((None,) + bias.shape[1:], lambda b, p: (p, 0, 0, 0, 0)),
                  pl.BlockSpec((None, POOL_GROUP_DIM, POOL_GROUP_DIM), lambda b, p: (p, 0, 0)),
                  pl.BlockSpec((None, 1, POOL_GROUP_DIM), lambda b, p: (p, 0, 0))],
        out_specs=[blk, blk],
        out_shape=[jax.ShapeDtypeStruct((m, A_WIDTH), F32), jax.ShapeDtypeStruct((m, POOL_WIDTH), F32)],
        scratch_shapes=[pltpu.VMEM((seq, LANE), F32), pltpu.VMEM((seq, LANE), F32),
                        pltpu.VMEM((seq, LANE), F32), pltpu.VMEM((POOL_PAD + seq, LANE), F32)],
        compiler_params=_cparams(("parallel", "arbitrary")),
        name="even_prompt",
    )(q, k, v, u, bias, w_pool, scale)


def _kv_decode_kernel(aliased, kt_ref, vt_ref, q_ref, kn_ref, vn_ref, *refs):
    ko_ref, vo_ref, o_ref = refs[2:] if aliased else refs
    hist = kt_ref.shape[1]
    pos = lax.broadcasted_iota(jnp.int32, (1, hist), 1)
    dist = hist - pos
    mult = jnp.zeros((1, hist), F32)
    for window, dil in A_BRANCHES:
        member = jnp.logical_and(dist <= window, jnp.bitwise_and(dist, dil - 1) == 0)
        mult = mult + member.astype(F32)
    valid = mult > 0.0
    dist_f = dist.astype(F32)
    n_branches = float(len(A_BRANCHES))
    for h in range(A_HEADS):
        col = slice(h, h + 1)
        slabs = [(slice(d, d + KV_SLAB), slice(h * HEAD_DIM + d, h * HEAD_DIM + d + KV_SLAB))
                 for d in range(0, HEAD_DIM, KV_SLAB)]
        qh = q_ref[:, col] * A_SCALE
        part = None
        for dsl, rsl in slabs:
            t = kt_ref[rsl, :] * qh[dsl]
            part = t if part is None else part + t
        s = jnp.sum(part, axis=0, keepdims=True) - _alibi_slope(h) * dist_f
        s = jnp.where(valid, s, NEG)
        s_new = jnp.sum(kn_ref[:, col] * qh, axis=0, keepdims=True)
        m = jnp.maximum(jnp.max(s, axis=-1, keepdims=True), s_new)
        p = mult * jnp.exp(s - m)
        p_new = n_branches * jnp.exp(s_new - m)
        inv_l = 1.0 / (jnp.sum(p, axis=-1, keepdims=True) + p_new)
        for dsl, rsl in slabs:
            pv = jnp.sum(vt_ref[rsl, :] * p, axis=-1, keepdims=True)
            o_ref[dsl, col] = (pv + vn_ref[dsl, col] * p_new) * inv_l

    last = slice(hist - LANE, hist)
    newest = lax.broadcasted_iota(jnp.int32, (KV_SLAB, LANE), 1) == LANE - 1
    for src_ref, new_ref, dst_ref in ((kt_ref, kn_ref, ko_ref), (vt_ref, vn_ref, vo_ref)):
        for h in range(A_HEADS):
            for d in range(0, HEAD_DIM, KV_SLAB):
                rsl = slice(h * HEAD_DIM + d, h * HEAD_DIM + d + KV_SLAB)
                rolled = pltpu.roll(src_ref[rsl, :], hist - 1, axis=1)
                dst_ref[rsl, 0:hist - LANE] = rolled[:, 0:hist - LANE]
                dst_ref[rsl, last] = jnp.where(newest, new_ref[d:d + KV_SLAB, h:h + 1], rolled[:, last])


def _kv_decode(layer, kt, vt, q_col, kn_col, vn_col, prev):
    _, nb, width, hist = kt.shape
    blk = pl.BlockSpec((None, None, width, hist), lambda b: (layer, b, 0, 0))
    col = pl.BlockSpec((None, HEAD_DIM, A_HEADS), lambda b: (b, 0, 0))
    in_specs = [blk, blk, col, col, col]
    args = [kt, vt, q_col, kn_col, vn_col]
    aliases = {}
    if prev is not None:
        in_specs += [pl.BlockSpec(memory_space=pl.ANY)] * 2
        aliases = {len(args): 0, len(args) + 1: 1}
        args += list(prev)
    return pl.pallas_call(
        functools.partial(_kv_decode_kernel, prev is not None),
        grid=(nb,),
        in_specs=in_specs,
        out_specs=[blk, blk, col],
        out_shape=[jax.ShapeDtypeStruct(kt.shape, F32), jax.ShapeDtypeStruct(vt.shape, F32),
                   jax.ShapeDtypeStruct((nb, HEAD_DIM, A_HEADS), F32)],
        input_output_aliases=aliases,
        compiler_params=_cparams(("arbitrary",)),
        name="kv_decode",
    )(*args)


def _pool_decode_kernel(u_ref, hist_ref, wp_ref, sc_ref, o_ref):
    u = u_ref[...]
    for g, w in enumerate(POOL_WINDOWS):
        cols = slice(g * POOL_GROUP_DIM, (g + 1) * POOL_GROUP_DIM)
        cur = u[:, cols]
        tot = cur
        for j in range(POOL_HIST - (w - 1), POOL_HIST):
            tot = tot + hist_ref[j, :, cols]
        pooled = tot / float(w) - cur
        o_ref[:, cols] = jnp.dot(pooled.astype(BF16), wp_ref[g], preferred_element_type=F32) * sc_ref[:, cols]


def _pool_decode(u, hist_t, w_pool, scale):
    nb = u.shape[0]
    return pl.pallas_call(
        _pool_decode_kernel,
        out_shape=jax.ShapeDtypeStruct((nb, POOL_WIDTH), F32),
        compiler_params=pltpu.CompilerParams(vmem_limit_bytes=VMEM_LIMIT),
        name="pool_decode",
    )(u, hist_t, w_pool, scale)


def _gates(ba, alog_ref, dtb_ref):
    beta = jax.nn.sigmoid(ba)
    g = -jnp.exp(alog_ref[...]) * _softplus(ba + dtb_ref[...])
    return beta, g


def _l2norm(x):
    return x * lax.rsqrt(jnp.sum(x * x, axis=-1, keepdims=True) + 1e-6)


def _gated_out(o, gate, nw):
    o = o * lax.rsqrt(jnp.mean(o * o, axis=-1, keepdims=True) + RMS_EPS) * nw
    return o * _silu(gate)


def _gdn_prompt_kernel(qkv_ref, halo_ref, gate_ref, ba_ref, wc_ref, alog_ref, dtb_ref, nw_ref,
                       o_ref, sfin_ref, s_scr, xx_scr):
    step = pl.program_id(1)
    rows = GDN_ROWS

    @pl.when(step == 0)
    def _():
        s_scr[...] = jnp.zeros(s_scr.shape, F32)

    xx_scr[0:HALO, :] = jnp.where(step == 0, 0.0, halo_ref[...])
    xx_scr[HALO:HALO + rows, :] = qkv_ref[...]

    def conv(c0):
        cols = slice(c0, c0 + GDN_DK)
        acc = wc_ref[CONV_WIDTH - 1:CONV_WIDTH, cols] * xx_scr[HALO:HALO + rows, cols]
        for i in range(1, CONV_WIDTH):
            acc = acc + wc_ref[CONV_WIDTH - 1 - i:CONV_WIDTH - i, cols] * xx_scr[HALO - i:HALO - i + rows, cols]
        return _silu(acc)

    beta_all, g_all = _gates(ba_ref[...], alog_ref, dtb_ref)
    row = lax.broadcasted_iota(jnp.int32, (rows, LANE), 0)
    in_chunk = jnp.bitwise_and(row, GDN_CHUNK - 1)
    gcs = g_all
    shift = 1
    while shift < GDN_CHUNK:
        gcs = gcs + jnp.where(in_chunk >= shift, pltpu.roll(gcs, shift, axis=0), 0.0)
        shift *= 2
    glast = jnp.concatenate(
        [jnp.broadcast_to(gcs[(c + 1) * GDN_CHUNK - 1:(c + 1) * GDN_CHUNK, :], (GDN_CHUNK, LANE))
         for c in range(GDN_NC)], axis=0)
    eg_all = jnp.exp(gcs)
    ekd_all = jnp.exp(glast - gcs)
    gcs_t = gcs.T
    eglast_t = jnp.exp(gcs_t)

    ri = lax.broadcasted_iota(jnp.int32, (rows, rows), 0)
    ci = lax.broadcasted_iota(jnp.int32, (rows, rows), 1)
    same = lax.shift_right_logical(ri, GDN_CHUNK_SHIFT) == lax.shift_right_logical(ci, GDN_CHUNK_SHIFT)
    bd_tril = jnp.logical_and(same, ri >= ci)
    bd_strict = jnp.logical_and(same, ri > ci)
    eye_cat = (lax.broadcasted_iota(jnp.int32, (GDN_CHUNK, rows), 0)
               == jnp.bitwise_and(lax.broadcasted_iota(jnp.int32, (GDN_CHUNK, rows), 1), GDN_CHUNK - 1)
               ).astype(F32)

    def block_of(x, size):
        return lax.shift_right_logical(x, size.bit_length() - 1)

    m_same = same.astype(BF16)
    m_base = (block_of(ri, GDN_BASE) == block_of(ci, GDN_BASE)).astype(BF16)
    m_merge = []
    size = GDN_BASE
    while size < GDN_CHUNK:
        pair = block_of(ri, 2 * size) == block_of(ci, 2 * size)
        lower_left = jnp.logical_and(jnp.bitwise_and(block_of(ri, size), 1) == 1,
                                     jnp.bitwise_and(block_of(ci, size), 1) == 0)
        m_merge.append(jnp.logical_and(pair, lower_left).astype(BF16))
        size *= 2

    def to_cat(x):
        out = x[0:GDN_CHUNK]
        for c in range(1, GDN_NC):
            out = out + x[c * GDN_CHUNK:(c + 1) * GDN_CHUNK]
        return out

    def to_bd(x):
        return jnp.concatenate([x.astype(BF16)] * GDN_NC, axis=0) * m_same

    def mm(a, b):
        return jnp.dot(a.astype(BF16), b, preferred_element_type=F32)

    def local(h):
        gi = GDN_HEADS + h
        beta = beta_all[:, h:h + 1]
        egc = eg_all[:, gi:gi + 1]
        q = _l2norm(conv(h * GDN_DK)) * (GDN_DK ** -0.5)
        k = _l2norm(conv(GDN_QK + h * GDN_DK))
        v = conv(2 * GDN_QK + h * GDN_DK)
        kb = k * beta
        prod = _dot_nt(jnp.concatenate([kb, q], axis=0), k)
        diff = gcs[:, gi:gi + 1] - gcs_t[gi:gi + 1, :]
        decay = jnp.where(bd_tril, jnp.exp(jnp.where(bd_tril, diff, 0.0)), 0.0)
        return dict(neg_l=jnp.where(bd_strict, -(prod[:rows] * decay), 0.0),
                    a_intra=(prod[rows:] * decay).astype(BF16),
                    rhs=jnp.concatenate([v * beta, kb * egc], axis=1).astype(BF16),
                    qdec=(q * egc).astype(BF16),
                    kdec=(k * ekd_all[:, gi:gi + 1]).astype(BF16))

    for g0 in range(0, GDN_HEADS, GDN_GROUP):
        heads = list(range(g0, g0 + GDN_GROUP))
        loc = [local(h) for h in heads]
        neg_l = [x["neg_l"].astype(BF16) for x in loc]
        p_bd = [n * m_base for n in neg_l]
        p_cat = [to_cat(p) for p in p_bd]
        inv = [eye_cat + p.astype(F32) for p in p_cat]
        p_cat = [mm(p, b) for p, b in zip(p_cat, p_bd)]
        power = 2
        while 2 * power < GDN_BASE:
            res = [mm(jnp.concatenate([p, t], axis=0), to_bd(p)) for p, t in zip(p_cat, inv)]
            p_cat = [r[:GDN_CHUNK] for r in res]
            inv = [t + r[GDN_CHUNK:] for t, r in zip(inv, res)]
            power *= 2
        inv = [t + mm(t, to_bd(p)) for p, t in zip(p_cat, inv)]
        for m_off in m_merge:
            half = [mm(t, n * m_off) for t, n in zip(inv, neg_l)]
            inv = [t + mm(y, to_bd(t)) for t, y in zip(inv, half)]
        uw = [jnp.dot(to_bd(t), x["rhs"], preferred_element_type=F32) for t, x in zip(inv, loc)]
        state = [s_scr[h] for h in heads]
        v_news = [[] for _ in heads]
        q_states = [[] for _ in heads]
        for c in range(GDN_NC):
            r = slice(c * GDN_CHUNK, (c + 1) * GDN_CHUNK)
            last = (c + 1) * GDN_CHUNK - 1
            for j, h in enumerate(heads):
                gi = GDN_HEADS + h
                lhs = jnp.concatenate([uw[j][r, GDN_DK:].astype(BF16), loc[j]["qdec"][r]], axis=0)
                res = jnp.dot(lhs, state[j].astype(BF16), preferred_element_type=F32)
                v_new = uw[j][r, :GDN_DK] - res[:GDN_CHUNK]
                q_states[j].append(res[GDN_CHUNK:])
                v_news[j].append(v_new)
                state[j] = state[j] * eglast_t[gi:gi + 1, last:last + 1] + _dot_tn(loc[j]["kdec"][r], v_new)
        for j, h in enumerate(heads):
            s_scr[h] = state[j]
            o = jnp.concatenate(q_states[j], axis=0) + jnp.dot(
                loc[j]["a_intra"], jnp.concatenate(v_news[j], axis=0).astype(BF16), preferred_element_type=F32)
            cols = slice(h * GDN_DK, (h + 1) * GDN_DK)
            o_ref[:, cols] = _gated_out(o, gate_ref[:, cols], nw_ref[...])

    @pl.when(step == pl.num_programs(1) - 1)
    def _():
        sfin_ref[...] = s_scr[...]


def _gdn_prompt(qkv, gate, ba, w_conv, alog, dtb, nw, batch, seq):
    m = batch * seq
    steps = seq // GDN_ROWS
    per_halo = GDN_ROWS // HALO
    row_blk = lambda b, c: (b * steps + c, 0)
    return pl.pallas_call(
        _gdn_prompt_kernel,
        grid=(batch, steps),
        in_specs=[pl.BlockSpec((GDN_ROWS, GDN_CONV_CH), row_blk),
                  pl.BlockSpec((HALO, GDN_CONV_CH),
                               lambda b, c: (jnp.maximum((b * steps + c) * per_halo - 1, 0), 0)),
                  pl.BlockSpec((GDN_ROWS, GDN_VW), row_blk),
                  pl.BlockSpec((GDN_ROWS, LANE), row_blk),
                  _resident(w_conv.shape), _resident(alog.shape), _resident(dtb.shape), _resident(nw.shape)],
        out_specs=[pl.BlockSpec((GDN_ROWS, GDN_VW), row_blk),
                   pl.BlockSpec((None, GDN_HEADS, GDN_DK, GDN_DK), lambda b, c: (b, 0, 0, 0))],
        out_shape=[jax.ShapeDtypeStruct((m, GDN_VW), F32),
                   jax.ShapeDtypeStruct((batch, GDN_HEADS, GDN_DK, GDN_DK), F32)],
        scratch_shapes=[pltpu.VMEM((GDN_HEADS, GDN_DK, GDN_DK), F32),
                        pltpu.VMEM((HALO + GDN_ROWS, GDN_CONV_CH), F32)],
        compiler_params=_cparams(("parallel", "arbitrary")),
        name="gdn_prompt",
    )(qkv, qkv, gate, ba, w_conv, alog, dtb, nw)


def _gdn_decode_prep_kernel(x_ref, hist_ref, ba_ref, wc_ref, alog_ref, dtb_ref, q_ref, k_ref, v_ref, bg_ref):
    for h in range(3 * GDN_HEADS):
        cols = slice(h * GDN_DK, (h + 1) * GDN_DK)
        acc = wc_ref[CONV_WIDTH - 1:CONV_WIDTH, cols] * x_ref[:, cols]
        for i in range(1, CONV_WIDTH):
            acc = acc + wc_ref[CONV_WIDTH - 1 - i:CONV_WIDTH - i, cols] * hist_ref[CONV_WIDTH - 1 - i, :, cols]
        y = _silu(acc)
        if h < GDN_HEADS:
            q_ref[:, cols] = _l2norm(y) * (GDN_DK ** -0.5)
        elif h < 2 * GDN_HEADS:
            k_ref[:, slice((h - GDN_HEADS) * GDN_DK, (h - GDN_HEADS + 1) * GDN_DK)] = _l2norm(y)
        else:
            v_ref[:, slice((h - 2 * GDN_HEADS) * GDN_DK, (h - 2 * GDN_HEADS + 1) * GDN_DK)] = y
    beta, g = _gates(ba_ref[...], alog_ref, dtb_ref)
    lane = lax.broadcasted_iota(jnp.int32, beta.shape, 1)
    bg_ref[...] = jnp.where(lane < GDN_HEADS, beta, g)


def _gdn_decode_prep(x, hist_t, ba, w_conv, alog, dtb):
    nb = x.shape[0]
    head = jax.ShapeDtypeStruct((nb, GDN_QK), F32)
    return pl.pallas_call(
        _gdn_decode_prep_kernel,
        out_shape=[head, head, head, jax.ShapeDtypeStruct((nb, LANE), F32)],
        compiler_params=pltpu.CompilerParams(vmem_limit_bytes=VMEM_LIMIT),
        name="gdn_decode_prep",
    )(x, hist_t, ba, w_conv, alog, dtb)


def _gdn_decode_kernel(aliased, q_ref, k_ref, v_ref, gate_ref, beta_ref, g_ref, s_ref, nw_ref, *refs):
    o_ref, so_ref = refs[1:] if aliased else refs
    pad = jnp.zeros((6, GDN_DK), F32)
    for t in range(q_ref.shape[0]):
        q_all, k_all, v_all = q_ref[t], k_ref[t], v_ref[t]
        beta_all = beta_ref[t]
        eg_all = jnp.exp(g_ref[t])
        qk_all = jnp.sum(q_all * k_all, axis=-1, keepdims=True)
        outs = []
        for h in range(GDN_HEADS):
            s = s_ref[t, h]
            q, k, v = q_all[h:h + 1], k_all[h:h + 1], v_all[h:h + 1]
            beta, eg = beta_all[h:h + 1], eg_all[h:h + 1]
            res = _dot(jnp.concatenate([k, q, pad], axis=0), s)
            v_new = beta * (v - eg * res[0:1])
            o = eg * res[1:2] + qk_all[h:h + 1] * v_new
            k_col = jnp.broadcast_to(k, (GDN_DK, GDN_DK)).T
            so_ref[t, h] = s * eg + k_col * v_new
            outs.append(o)
        o_ref[t] = _gated_out(jnp.concatenate(outs, axis=0), gate_ref[t], nw_ref[...])


def _gdn_decode(layer, q3, k3, v3, gate3, beta3, g3, state, nw, prev):
    nb = q3.shape[0]
    tb = GDN_DECODE_TOKENS
    tok = pl.BlockSpec((tb, GDN_HEADS, GDN_DK), lambda b: (b, 0, 0))
    sc = pl.BlockSpec((tb, GDN_HEADS, 1), lambda b: (b, 0, 0))
    st = pl.BlockSpec((None, tb, GDN_HEADS, GDN_DK, GDN_DK), lambda b: (layer, b, 0, 0, 0))
    in_specs = [tok, tok, tok, tok, sc, sc, st, _resident(nw.shape)]
    args = [q3, k3, v3, gate3, beta3, g3, state, nw]
    aliases = {}
    if prev is not None:
        in_specs.append(pl.BlockSpec(memory_space=pl.ANY))
        aliases = {len(args): 1}
        args.append(prev)
    return pl.pallas_call(
        functools.partial(_gdn_decode_kernel, prev is not None),
        grid=(nb // tb,),
        in_specs=in_specs,
        out_specs=[tok, st],
        out_shape=[jax.ShapeDtypeStruct((nb, GDN_HEADS, GDN_DK), F32), jax.ShapeDtypeStruct(state.shape, F32)],
        input_output_aliases=aliases,
        compiler_params=_cparams(("arbitrary",)),
        name="gdn_decode",
    )(*args)


def _pad_cols(x, start, width=LANE):
    return jnp.pad(x, ((0, 0), (start, width - start - x.shape[1])))


def _positions_minor(x):
    lead = x.shape[:-3]
    n = len(lead)
    xt = x.transpose(*range(n), n + 1, n + 2, n)
    return xt.reshape(*lead, x.shape[-2] * x.shape[-1], x.shape[-3])


def _positions_major(xt, heads):
    lead = xt.shape[:-2]
    n = len(lead)
    x = xt.reshape(*lead, heads, xt.shape[-2] // heads, xt.shape[-1])
    return x.transpose(*range(n), n + 2, n, n + 1)


def kernel(x_prompt, x_sample, state_a_k, state_a_v, state_pool, state_conv, state_gdn, norm_mix, norm_ffn,
           w_in_a, w_out_a, w_pool, pool_scale, w_in_c, w_conv_c, a_log_c, dt_bias_c, norm_out_c, w_out_c,
           w_ffn_gate, w_ffn_up, w_ffn_down, norm_final):
    batch, seq, d = x_prompt.shape
    nb = x_sample.shape[0]
    depth = norm_mix.shape[0]
    tm = 512

    w_in_a_b, w_out_a_b, w_pool_b = w_in_a.astype(BF16), w_out_a.astype(BF16), w_pool.astype(BF16)
    w_in_c_b, w_out_c_b = w_in_c.astype(BF16), w_out_c.astype(BF16)
    wg_b, wu_b, wd_b = w_ffn_gate.astype(BF16), w_ffn_up.astype(BF16), w_ffn_down.astype(BF16)
    w_ba = jnp.stack([_pad_cols(w_in_c_b[i][:, GDN_CONV_CH + GDN_VW:], 0) for i in range(w_in_c.shape[0])])

    def ws_even(i):
        return [(w_in_a_b, (None, d, A_WIDTH), (i, 0, j)) for j in range(4)]

    def ws_odd(i):
        return [(w_in_c_b, (None, d, GDN_CONV_CH), (i, 0, 0)),
                (w_in_c_b, (None, d, GDN_VW), (i, 0, GDN_CONV_CH // GDN_VW)),
                (w_ba, (None, d, LANE), (i, 0, 0))]

    def wos_even(i):
        return [(w_out_a_b, (None, A_WIDTH, d), (i, 0, 0)), (w_out_a_b, (None, POOL_WIDTH, d), (i, 1, 0))]

    def wos_odd(i):
        return [(w_out_c_b, (None, GDN_VW, d), (i, 0, 0))]

    alog_p = _pad_cols(a_log_c, GDN_HEADS)
    dtb_p = _pad_cols(dt_bias_c, GDN_HEADS)
    bias = _attn_bias()
    g_final = norm_final.reshape(1, d)
    n_even = w_in_a.shape[0]

    def ffn_args(layer):
        last = layer == depth - 1
        return (norm_ffn[layer:layer + 1], (wg_b, (None, d, FFN_HIDDEN), (layer, 0, 0)),
                (wu_b, (None, d, FFN_HIDDEN), (layer, 0, 0)), (wd_b, (None, FFN_HIDDEN, d), (layer, 0, 0)),
                g_final if last else None)

    xs = x_sample.reshape(nb, d)
    kt_hist, vt_hist = _positions_minor(state_a_k), _positions_minor(state_a_v)
    kv_out = gdn_out = None
    pool_s, conv_s = [], []
    for layer in range(depth):
        i = layer // 2
        g_mix = norm_mix[layer:layer + 1]
        if layer % 2 == 0:
            q, k, v, u = _norm_proj(xs, g_mix, ws_even(i), nb)
            as_col = lambda a: a.reshape(nb, A_HEADS, HEAD_DIM).transpose(0, 2, 1)
            k_out, v_out, attn = _kv_decode(i, kt_hist, vt_hist, as_col(q), as_col(k), as_col(v), kv_out)
            kv_out = (k_out, v_out)
            pooled = _pool_decode(u, state_pool[i].transpose(1, 0, 2), w_pool_b[i], pool_scale[i:i + 1])
            pool_s.append(jnp.concatenate([state_pool[i][:, 1:], u[:, None, :]], axis=1))
            mixes = [attn.transpose(0, 2, 1).reshape(nb, A_WIDTH), pooled]
            wos = wos_even(i)
        else:
            qkv, gate, ba = _norm_proj(xs, g_mix, ws_odd(i), nb)
            qn, kn, vv, bg = _gdn_decode_prep(qkv, state_conv[i].transpose(1, 0, 2), ba, w_conv_c[i],
                                              alog_p[i:i + 1], dtb_p[i:i + 1])
            shape3 = (nb, GDN_HEADS, GDN_DK)
            o3, gdn_out = _gdn_decode(i, qn.reshape(shape3), kn.reshape(shape3), vv.reshape(shape3),
                                      gate.reshape(shape3), bg[:, :GDN_HEADS].reshape(nb, GDN_HEADS, 1),
                                      bg[:, GDN_HEADS:2 * GDN_HEADS].reshape(nb, GDN_HEADS, 1),
                                      state_gdn, norm_out_c[i:i + 1], gdn_out)
            conv_s.append(jnp.concatenate([state_conv[i][:, 1:], qkv[:, None, :]], axis=1))
            mixes = [o3.reshape(nb, GDN_VW)]
            wos = wos_odd(i)
        xs = _mix_ffn(xs, mixes, wos, *ffn_args(layer), nb)
    y_sample = xs.reshape(nb, 1, d)
    a_k_sample = _positions_major(kv_out[0], A_HEADS)
    a_v_sample = _positions_major(kv_out[1], A_HEADS)

    xp = x_prompt.reshape(batch * seq, d)
    kvt_p = None
    pool_p, conv_p, gdn_p = [], [], []
    for layer in range(depth):
        i = layer // 2
        g_mix = norm_mix[layer:layer + 1]
        if layer % 2 == 0:
            q, k, v, u, kt, vt = _even_in_proj(xp, g_mix, ws_even(i), i, n_even, kvt_p, batch, seq, tm)
            kvt_p = (kt, vt)
            attn, pooled = _even_prompt(q, k, v, u, bias, w_pool_b[i],
                                        pool_scale[i].reshape(len(POOL_WINDOWS), 1, POOL_GROUP_DIM), batch, seq)
            pool_p.append(u.reshape(batch, seq, POOL_WIDTH)[:, seq - POOL_HIST:])
            mixes = [attn, pooled]
            wos = wos_even(i)
        else:
            qkv, gate, ba = _norm_proj(xp, g_mix, ws_odd(i), tm)
            o, s_fin = _gdn_prompt(qkv, gate, ba, w_conv_c[i], alog_p[i:i + 1], dtb_p[i:i + 1],
                                   norm_out_c[i:i + 1], batch, seq)
            conv_p.append(qkv.reshape(batch, seq, GDN_CONV_CH)[:, seq - (CONV_WIDTH - 1):])
            gdn_p.append(s_fin)
            mixes = [o]
            wos = wos_odd(i)
        xp = _mix_ffn(xp, mixes, wos, *ffn_args(layer), tm)
    y_prompt = xp.reshape(batch, seq, d)
    a_k_prompt = _positions_major(kvt_p[0], A_HEADS)
    a_v_prompt = _positions_major(kvt_p[1], A_HEADS)

    return (y_prompt, y_sample, a_k_prompt, a_v_prompt, jnp.stack(pool_p), jnp.stack(conv_p),
            jnp.stack(gdn_p), a_k_sample, a_v_sample, jnp.stack(pool_s), jnp.stack(conv_s), gdn_out)
```

```python
import functools

import jax
import jax.numpy as jnp
from jax import lax
from jax.experimental import pallas as pl
from jax.experimental.pallas import tpu as pltpu

F32 = jnp.float32
BF16 = jnp.bfloat16

D_MODEL = 1024
RMS_EPS = 1e-6
NEG = -1e30

HEAD_DIM = 64
A_HEADS = 8
A_WIDTH = A_HEADS * HEAD_DIM
A_BRANCHES = ((128, 1), (512, 4), (2048, 16))
A_DILS = tuple(d for _, d in A_BRANCHES)
A_KEYS = 128
A_BLOCK = 128
A_SCALE = HEAD_DIM ** -0.5
N_PAIRS = A_HEADS // 2
A_INFLIGHT = 8
KV_SLAB = 8

POOL_WINDOWS = (2, 4, 8, 16)
POOL_WIDTH = 512
POOL_GROUP_DIM = 128
POOL_HIST = 15
POOL_PAD = 16

GDN_HEADS = 8
GDN_DK = 128
GDN_QK = GDN_HEADS * GDN_DK
GDN_VW = GDN_HEADS * GDN_DK
GDN_CONV_CH = 3 * GDN_QK
CONV_WIDTH = 4
GDN_CHUNK = 64
GDN_CHUNK_SHIFT = GDN_CHUNK.bit_length() - 1
GDN_ROWS = 256
GDN_NC = GDN_ROWS // GDN_CHUNK
GDN_GROUP = 8
GDN_BASE = 8
HALO = 8
GDN_DECODE_TOKENS = 4

FFN_HIDDEN = 2816
FFN_CHUNK = 256
LANE = 128

VMEM_LIMIT = 56 * 1024 * 1024


def _cparams(sem):
    return pltpu.CompilerParams(dimension_semantics=sem, vmem_limit_bytes=VMEM_LIMIT)


def _resident(shape):
    nd = len(shape)
    return pl.BlockSpec(shape, lambda *_: (0,) * nd, pipeline_mode=pl.Buffered(1))


def _resident_part(block, index):
    return pl.BlockSpec(block, lambda *_: index, pipeline_mode=pl.Buffered(1))


def _rms(x, g):
    ms = jnp.mean(x * x, axis=-1, keepdims=True)
    return x * lax.rsqrt(ms + RMS_EPS) * g


def _silu(x):
    return x * jax.nn.sigmoid(x)


def _softplus(x):
    return jnp.maximum(x, 0.0) + jnp.log1p(jnp.exp(-jnp.abs(x)))


def _dot(a, b):
    return jnp.dot(a.astype(BF16), b.astype(BF16), preferred_element_type=F32)


def _dot_nt(a, b):
    return lax.dot_general(a.astype(BF16), b.astype(BF16), (((1,), (1,)), ((), ())),
                           preferred_element_type=F32)


def _dot_tn(a, b):
    return lax.dot_general(a.astype(BF16), b.astype(BF16), (((0,), (0,)), ((), ())),
                           preferred_element_type=F32)


def _norm_proj_kernel(n_out, x_ref, g_ref, *refs):
    w_refs, o_refs = refs[:n_out], refs[n_out:]
    h = _rms(x_ref[...], g_ref[...]).astype(BF16)
    for w_ref, o_ref in zip(w_refs, o_refs):
        n = w_ref.shape[1]
        for c0 in range(0, n, 512):
            c1 = min(c0 + 512, n)
            o_ref[:, c0:c1] = jnp.dot(h, w_ref[:, c0:c1], preferred_element_type=F32)


def _norm_proj(x, g, ws, tm):
    m, d = x.shape
    n_out = len(ws)
    widths = [block[-1] for _, block, _ in ws]
    return pl.pallas_call(
        functools.partial(_norm_proj_kernel, n_out),
        grid=(m // tm,),
        in_specs=[pl.BlockSpec((tm, d), lambda i: (i, 0)), _resident((1, d))]
        + [_resident_part(block, index) for _, block, index in ws],
        out_specs=[pl.BlockSpec((tm, n), lambda i: (i, 0)) for n in widths],
        out_shape=[jax.ShapeDtypeStruct((m, n), F32) for n in widths],
        compiler_params=_cparams(("parallel",)),
        name="norm_proj",
    )(x, g, *[w for w, _, _ in ws])


def _even_in_proj_kernel(aliased, x_ref, g_ref, wq_ref, wk_ref, wv_ref, wu_ref, *refs):
    q_ref, k_ref, v_ref, u_ref, kt_ref, vt_ref = refs[2:] if aliased else refs
    h = _rms(x_ref[...], g_ref[...]).astype(BF16)
    q_ref[...] = jnp.dot(h, wq_ref[...], preferred_element_type=F32)
    u_ref[...] = jnp.dot(h, wu_ref[...], preferred_element_type=F32)
    k = jnp.dot(h, wk_ref[...], preferred_element_type=F32)
    k_ref[...] = k
    kt_ref[...] = k.T
    v = jnp.dot(h, wv_ref[...], preferred_element_type=F32)
    v_ref[...] = v
    vt_ref[...] = v.T


def _even_in_proj(x, g, ws, slot, n_slots, prev, batch, seq, tm):
    m, d = x.shape
    per_seq = seq // tm
    row = pl.BlockSpec((tm, A_WIDTH), lambda i: (i, 0))
    tr = pl.BlockSpec((None, None, A_WIDTH, tm), lambda i: (slot, i // per_seq, 0, i % per_seq))
    flat = jax.ShapeDtypeStruct((m, A_WIDTH), F32)
    trs = jax.ShapeDtypeStruct((n_slots, batch, A_WIDTH, seq), F32)
    in_specs = [pl.BlockSpec((tm, d), lambda i: (i, 0)), _resident((1, d))]
    in_specs += [_resident_part(block, index) for _, block, index in ws]
    args = [x, g] + [w for w, _, _ in ws]
    aliases = {}
    if prev is not None:
        in_specs += [pl.BlockSpec(memory_space=pl.ANY)] * 2
        aliases = {len(args): 4, len(args) + 1: 5}
        args += list(prev)
    return pl.pallas_call(
        functools.partial(_even_in_proj_kernel, prev is not None),
        grid=(m // tm,),
        in_specs=in_specs,
        out_specs=[row, row, row, row, tr, tr],
        out_shape=[flat, flat, flat, flat, trs, trs],
        input_output_aliases=aliases,
        compiler_params=_cparams(("arbitrary",)),
        name="even_in_proj",
    )(*args)


def _mix_ffn_kernel(n_mix, final, x_ref, *refs):
    a_refs = refs[:n_mix]
    wo_refs = refs[n_mix:2 * n_mix]
    g_ref, wg_ref, wu_ref, wd_ref = refs[2 * n_mix:2 * n_mix + 4]
    rest = refs[2 * n_mix + 4:]
    if final:
        gf_ref, o_ref, act_ref = rest
    else:
        o_ref, act_ref = rest
    x1 = x_ref[...]
    for a_ref, wo_ref in zip(a_refs, wo_refs):
        x1 = x1 + jnp.dot(a_ref[...].astype(BF16), wo_ref[...], preferred_element_type=F32)
    h = _rms(x1, g_ref[...]).astype(BF16)
    for c0 in range(0, FFN_HIDDEN, FFN_CHUNK):
        gate = jnp.dot(h, wg_ref[:, c0:c0 + FFN_CHUNK], preferred_element_type=F32)
        up = jnp.dot(h, wu_ref[:, c0:c0 + FFN_CHUNK], preferred_element_type=F32)
        act_ref[:, c0:c0 + FFN_CHUNK] = (_silu(gate) * up).astype(BF16)
    out = x1 + jnp.dot(act_ref[...], wd_ref[...], preferred_element_type=F32)
    if final:
        out = _rms(out, gf_ref[...])
    o_ref[...] = out


def _mix_ffn(x, mixes, wos, g, wg, wu, wd, g_final, tm):
    m, d = x.shape
    n_mix = len(mixes)
    final = g_final is not None
    in_specs = [pl.BlockSpec((tm, d), lambda i: (i, 0))]
    in_specs += [pl.BlockSpec((tm, a.shape[1]), lambda i: (i, 0)) for a in mixes]
    in_specs += [_resident_part(block, index) for _, block, index in wos]
    in_specs += [_resident(g.shape)] + [_resident_part(block, index) for _, block, index in (wg, wu, wd)]
    args = [x, *mixes, *[w for w, _, _ in wos], g, *[w for w, _, _ in (wg, wu, wd)]]
    if final:
        in_specs.append(_resident(g_final.shape))
        args.append(g_final)
    return pl.pallas_call(
        functools.partial(_mix_ffn_kernel, n_mix, final),
        grid=(m // tm,),
        in_specs=in_specs,
        out_specs=pl.BlockSpec((tm, d), lambda i: (i, 0)),
        out_shape=jax.ShapeDtypeStruct((m, d), F32),
        scratch_shapes=[pltpu.VMEM((tm, FFN_HIDDEN), BF16)],
        compiler_params=_cparams(("parallel",)),
        name="mix_ffn",
    )(*args)


def _alibi_slope(h):
    return 2.0 ** (-8.0 * (h + 1.0) / A_HEADS)


def _attn_bias():
    rel = jnp.arange(A_BLOCK)[:, None] + A_BLOCK - jnp.arange(2 * A_BLOCK)[None, :]
    band = (rel >= 0) & (rel <= A_KEYS)
    slopes = jnp.exp2(-8.0 * (jnp.arange(A_HEADS, dtype=F32) + 1.0) / A_HEADS)
    dils = jnp.asarray(A_DILS, F32)
    bias = -slopes[None, :, None, None] * (rel.astype(F32)[None, None] * dils[:, None, None, None])
    bias = jnp.where(band[None, None], bias, NEG)
    bias = bias.reshape(len(A_DILS), N_PAIRS, 2, A_BLOCK, 2 * A_BLOCK)
    return bias.transpose(1, 0, 2, 3, 4)


def _attn_scores(items, bias_ref, d_idx):
    lane = lax.broadcasted_iota(jnp.int32, (A_BLOCK, LANE), 1)
    lo = lane < HEAD_DIM
    each = [(i, h) for i in range(len(items)) for h in range(2)]
    kbs = [kk.astype(BF16) for _, kk, _, _ in items]
    vbs = [vv.astype(BF16) for _, _, vv, _ in items]
    qms = {(i, h): jnp.where(lo if h == 0 else jnp.logical_not(lo), items[i][0] * A_SCALE, 0.0).astype(BF16)
           for i, h in each}
    ss = {(i, h): _dot_nt(qms[i, h], kbs[i]) for i, h in each}
    ss = {(i, h): ss[i, h] + (bias_ref[d_idx, h, :, A_BLOCK:] if items[i][3] else bias_ref[d_idx, h])
          for i, h in each}
    ms = {ih: jnp.max(ss[ih], axis=-1, keepdims=True) for ih in each}
    ps = {ih: jnp.exp(ss[ih] - ms[ih]) for ih in each}
    ls = {ih: jnp.sum(ps[ih], axis=-1, keepdims=True) for ih in each}
    pvs = {(i, h): jnp.dot(ps[i, h].astype(BF16), vbs[i], preferred_element_type=F32) for i, h in each}
    return [tuple(jnp.where(lo, x[i, 0], x[i, 1]) for x in (ms, ls, pvs)) for i in range(len(items))]


def _even_prompt_kernel(seq, q_ref, k_ref, v_ref, u_ref, bias_ref, wp_ref, sc_ref,
                        attn_ref, pool_ref, m_scr, l_scr, acc_scr, pad_scr):
    def rows(dil, start):
        if dil == 1:
            return pl.ds(start, A_BLOCK)
        return pl.ds(start, A_BLOCK, stride=dil)

    def load(dil, start, first):
        cur = rows(dil, start)
        q = q_ref[cur, :]
        if first:
            kk, vv = k_ref[cur, :], v_ref[cur, :]
        elif dil == 1:
            win = pl.ds(start - A_BLOCK, 2 * A_BLOCK)
            kk, vv = k_ref[win, :], v_ref[win, :]
        else:
            prev = rows(dil, start - A_BLOCK * dil)
            kk = jnp.concatenate([k_ref[prev, :], k_ref[cur, :]], axis=0)
            vv = jnp.concatenate([v_ref[prev, :], v_ref[cur, :]], axis=0)
        return cur, (q, kk, vv, first)

    def blocks(d_idx, dil, todo):
        loaded = [load(dil, start, first) for start, first in todo]
        stats = _attn_scores([item for _, item in loaded], bias_ref, d_idx)
        if d_idx != first_branch:
            old = [(m_scr[cur, :], l_scr[cur, :], acc_scr[cur, :]) for cur, _ in loaded]
            merged = []
            for (m, l, acc), (m0, l0, a0) in zip(stats, old):
                mx = jnp.maximum(m0, m)
                w0 = jnp.exp(m0 - mx)
                w1 = jnp.exp(m - mx)
                merged.append((mx, w0 * l0 + w1 * l, w0 * a0 + w1 * acc))
            stats = merged
        for (cur, _), (m, l, acc) in zip(loaded, stats):
            m_scr[cur, :] = m
            l_scr[cur, :] = l
            acc_scr[cur, :] = acc

    first_branch = len(A_DILS) - 1
    for d_idx, dil in reversed(list(enumerate(A_DILS))):
        span = A_BLOCK * dil
        n_blocks = seq // span
        if dil == 1:
            blocks(d_idx, dil, [(0, True)])
            group = 5

            def later(g, c, d_idx=d_idx, dil=dil, span=span, group=group):
                blocks(d_idx, dil, [((1 + g * group + j) * span, False) for j in range(group)])
                return c

            lax.fori_loop(0, (n_blocks - 1) // group, later, 0)
        elif n_blocks > 1:
            def residue(r, c, d_idx=d_idx, dil=dil, span=span, n_blocks=n_blocks):
                blocks(d_idx, dil, [(r, True)] + [(n * span + r, False) for n in range(1, n_blocks)])
                return c

            lax.fori_loop(0, dil, residue, 0)
        else:
            def residues(g, c, d_idx=d_idx, dil=dil):
                blocks(d_idx, dil, [(g * A_INFLIGHT + j, True) for j in range(A_INFLIGHT)])
                return c

            lax.fori_loop(0, dil // A_INFLIGHT, residues, 0)

    attn_ref[...] = (acc_scr[...] / l_scr[...]).astype(attn_ref.dtype)

    p_idx = pl.program_id(1)
    u = u_ref[...]
    pad_scr[0:POOL_PAD, :] = jnp.zeros((POOL_PAD, LANE), F32)
    s = u
    shift = 1
    for level in range(len(POOL_WINDOWS)):
        pad_scr[POOL_PAD:POOL_PAD + seq, :] = s
        nxt = s + pad_scr[POOL_PAD - shift:POOL_PAD - shift + seq, :]
        s = nxt if level == 0 else jnp.where(p_idx >= level, nxt, s)
        shift *= 2
    pos = lax.broadcasted_iota(jnp.int32, (seq, 1), 0)
    window = lax.shift_left(jnp.int32(2), p_idx)
    cnt = jnp.minimum(pos + 1, window).astype(F32)
    pooled = s / cnt - u
    y = jnp.dot(pooled.astype(BF16), wp_ref[...], preferred_element_type=F32) * sc_ref[...]
    pool_ref[...] = y.astype(pool_ref.dtype)


def _even_prompt(q, k, v, u, bias, w_pool, scale, batch, seq):
    m = batch * seq
    assert (seq // A_BLOCK - 1) % 5 == 0 and seq % (A_BLOCK * max(A_DILS)) == 0
    assert all(dil % A_INFLIGHT == 0 for dil in A_DILS if seq == A_BLOCK * dil)
    col = lambda b, p: (b, p)
    blk = pl.BlockSpec((seq, LANE), col)
    return pl.pallas_call(
        functools.partial(_even_prompt_kernel, seq),
        grid=(batch, N_PAIRS),
        in_specs=[blk, blk, blk, blk,
                  pl.BlockSpec((None,) + bias.shape[1:], lambda b, p: (p, 0, 0, 0, 0)),
                  pl.BlockSpec((None, POOL_GROUP_DIM, POOL_GROUP_DIM), lambda b, p: (p, 0, 0)),
                  pl.BlockSpec((None, 1, POOL_GROUP_DIM), lambda b, p: (p, 0, 0))],
        out_specs=[blk, blk],
        out_shape=[jax.ShapeDtypeStruct((m, A_WIDTH), BF16), jax.ShapeDtypeStruct((m, POOL_WIDTH), BF16)],
        scratch_shapes=[pltpu.VMEM((seq, LANE), F32), pltpu.VMEM((seq, LANE), F32),
                        pltpu.VMEM((seq, LANE), F32), pltpu.VMEM((POOL_PAD + seq, LANE), F32)],
        compiler_params=_cparams(("parallel", "arbitrary")),
        name="even_prompt",
    )(q, k, v, u, bias, w_pool, scale)


def _kv_decode_kernel(aliased, kt_ref, vt_ref, q_ref, kn_ref, vn_ref, *refs):
    ko_ref, vo_ref, o_ref = refs[2:] if aliased else refs
    hist = kt_ref.shape[1]
    pos = lax.broadcasted_iota(jnp.int32, (1, hist), 1)
    dist = hist - pos
    mult = jnp.zeros((1, hist), F32)
    for window, dil in A_BRANCHES:
        member = jnp.logical_and(dist <= window, jnp.bitwise_and(dist, dil - 1) == 0)
        mult = mult + member.astype(F32)
    valid = mult > 0.0
    dist_f = dist.astype(F32)
    n_branches = float(len(A_BRANCHES))
    for h in range(A_HEADS):
        col = slice(h, h + 1)
        slabs = [(slice(d, d + KV_SLAB), slice(h * HEAD_DIM + d, h * HEAD_DIM + d + KV_SLAB))
                 for d in range(0, HEAD_DIM, KV_SLAB)]
        qh = q_ref[:, col] * A_SCALE
        part = None
        for dsl, rsl in slabs:
            t = kt_ref[rsl, :] * qh[dsl]
            part = t if part is None else part + t
        s = jnp.sum(part, axis=0, keepdims=True) - _alibi_slope(h) * dist_f
        s = jnp.where(valid, s, NEG)
        s_new = jnp.sum(kn_ref[:, col] * qh, axis=0, keepdims=True)
        m = jnp.maximum(jnp.max(s, axis=-1, keepdims=True), s_new)
        p = mult * jnp.exp(s - m)
        p_new = n_branches * jnp.exp(s_new - m)
        inv_l = 1.0 / (jnp.sum(p, axis=-1, keepdims=True) + p_new)
        for dsl, rsl in slabs:
            pv = jnp.sum(vt_ref[rsl, :] * p, axis=-1, keepdims=True)
            o_ref[dsl, col] = (pv + vn_ref[dsl, col] * p_new) * inv_l

    last = slice(hist - LANE, hist)
    newest = lax.broadcasted_iota(jnp.int32, (KV_SLAB, LANE), 1) == LANE - 1
    for src_ref, new_ref, dst_ref in ((kt_ref, kn_ref, ko_ref), (vt_ref, vn_ref, vo_ref)):
        for h in range(A_HEADS):
            for d in range(0, HEAD_DIM, KV_SLAB):
                rsl = slice(h * HEAD_DIM + d, h * HEAD_DIM + d + KV_SLAB)
                rolled = pltpu.roll(src_ref[rsl, :], hist - 1, axis=1)
                dst_ref[rsl, 0:hist - LANE] = rolled[:, 0:hist - LANE]
                dst_ref[rsl, last] = jnp.where(newest, new_ref[d:d + KV_SLAB, h:h + 1], rolled[:, last])


def _kv_decode(layer, kt, vt, q_col, kn_col, vn_col, prev):
    _, nb, width, hist = kt.shape
    blk = pl.BlockSpec((None, None, width, hist), lambda b: (layer, b, 0, 0))
    col = pl.BlockSpec((None, HEAD_DIM, A_HEADS), lambda b: (b, 0, 0))
    in_specs = [blk, blk, col, col, col]
    args = [kt, vt, q_col, kn_col, vn_col]
    aliases = {}
    if prev is not None:
        in_specs += [pl.BlockSpec(memory_space=pl.ANY)] * 2
        aliases = {len(args): 0, len(args) + 1: 1}
        args += list(prev)
    return pl.pallas_call(
        functools.partial(_kv_decode_kernel, prev is not None),
        grid=(nb,),
        in_specs=in_specs,
        out_specs=[blk, blk, col],
        out_shape=[jax.ShapeDtypeStruct(kt.shape, F32), jax.ShapeDtypeStruct(vt.shape, F32),
                   jax.ShapeDtypeStruct((nb, HEAD_DIM, A_HEADS), F32)],
        input_output_aliases=aliases,
        compiler_params=_cparams(("arbitrary",)),
        name="kv_decode",
    )(*args)


def _pool_decode_kernel(u_ref, hist_ref, wp_ref, sc_ref, o_ref):
    u = u_ref[...]
    for g, w in enumerate(POOL_WINDOWS):
        cols = slice(g * POOL_GROUP_DIM, (g + 1) * POOL_GROUP_DIM)
        cur = u[:, cols]
        tot = cur
        for j in range(POOL_HIST - (w - 1), POOL_HIST):
            tot = tot + hist_ref[j, :, cols]
        pooled = tot / float(w) - cur
        o_ref[:, cols] = jnp.dot(pooled.astype(BF16), wp_ref[g], preferred_element_type=F32) * sc_ref[:, cols]


def _pool_decode(u, hist_t, w_pool, scale):
    nb = u.shape[0]
    return pl.pallas_call(
        _pool_decode_kernel,
        out_shape=jax.ShapeDtypeStruct((nb, POOL_WIDTH), F32),
        compiler_params=pltpu.CompilerParams(vmem_limit_bytes=VMEM_LIMIT),
        name="pool_decode",
    )(u, hist_t, w_pool, scale)


def _gates(ba, alog_ref, dtb_ref):
    beta = jax.nn.sigmoid(ba)
    g = -jnp.exp(alog_ref[...]) * _softplus(ba + dtb_ref[...])
    return beta, g


def _l2norm(x):
    return x * lax.rsqrt(jnp.sum(x * x, axis=-1, keepdims=True) + 1e-6)


def _gated_out(o, gate, nw):
    o = o * lax.rsqrt(jnp.mean(o * o, axis=-1, keepdims=True) + RMS_EPS) * nw
    return o * _silu(gate)


def _gdn_prompt_kernel(qkv_ref, halo_ref, gate_ref, ba_ref, wc_ref, alog_ref, dtb_ref, nw_ref,
                       o_ref, sfin_ref, s_scr, xx_scr):
    step = pl.program_id(1)
    rows = GDN_ROWS

    @pl.when(step == 0)
    def _():
        s_scr[...] = jnp.zeros(s_scr.shape, F32)

    xx_scr[0:HALO, :] = jnp.where(step == 0, 0.0, halo_ref[...])
    xx_scr[HALO:HALO + rows, :] = qkv_ref[...]

    def conv(c0):
        cols = slice(c0, c0 + GDN_DK)
        acc = wc_ref[CONV_WIDTH - 1:CONV_WIDTH, cols] * xx_scr[HALO:HALO + rows, cols]
        for i in range(1, CONV_WIDTH):
            acc = acc + wc_ref[CONV_WIDTH - 1 - i:CONV_WIDTH - i, cols] * xx_scr[HALO - i:HALO - i + rows, cols]
        return _silu(acc)

    beta_all, g_all = _gates(ba_ref[...], alog_ref, dtb_ref)
    row = lax.broadcasted_iota(jnp.int32, (rows, LANE), 0)
    in_chunk = jnp.bitwise_and(row, GDN_CHUNK - 1)
    gcs = g_all
    shift = 1
    while shift < GDN_CHUNK:
        gcs = gcs + jnp.where(in_chunk >= shift, pltpu.roll(gcs, shift, axis=0), 0.0)
        shift *= 2
    glast = jnp.concatenate(
        [jnp.broadcast_to(gcs[(c + 1) * GDN_CHUNK - 1:(c + 1) * GDN_CHUNK, :], (GDN_CHUNK, LANE))
         for c in range(GDN_NC)], axis=0)
    eg_all = jnp.exp(gcs)
    ekd_all = jnp.exp(glast - gcs)
    gcs_t = gcs.T
    eglast_t = jnp.exp(gcs_t)

    ri = lax.broadcasted_iota(jnp.int32, (rows, rows), 0)
    ci = lax.broadcasted_iota(jnp.int32, (rows, rows), 1)
    same = lax.shift_right_logical(ri, GDN_CHUNK_SHIFT) == lax.shift_right_logical(ci, GDN_CHUNK_SHIFT)
    bd_tril = jnp.logical_and(same, ri >= ci)
    bd_strict = jnp.logical_and(same, ri > ci)
    eye_cat = (lax.broadcasted_iota(jnp.int32, (GDN_CHUNK, rows), 0)
               == jnp.bitwise_and(lax.broadcasted_iota(jnp.int32, (GDN_CHUNK, rows), 1), GDN_CHUNK - 1)
               ).astype(F32)

    def block_of(x, size):
        return lax.shift_right_logical(x, size.bit_length() - 1)

    m_same = same.astype(BF16)
    m_base = (block_of(ri, GDN_BASE) == block_of(ci, GDN_BASE)).astype(BF16)
    m_merge = []
    size = GDN_BASE
    while size < GDN_CHUNK:
        pair = block_of(ri, 2 * size) == block_of(ci, 2 * size)
        lower_left = jnp.logical_and(jnp.bitwise_and(block_of(ri, size), 1) == 1,
                                     jnp.bitwise_and(block_of(ci, size), 1) == 0)
        m_merge.append(jnp.logical_and(pair, lower_left).astype(BF16))
        size *= 2

    def to_cat(x):
        out = x[0:GDN_CHUNK]
        for c in range(1, GDN_NC):
            out = out + x[c * GDN_CHUNK:(c + 1) * GDN_CHUNK]
        return out

    def to_bd(x):
        return jnp.concatenate([x.astype(BF16)] * GDN_NC, axis=0) * m_same

    def mm(a, b):
        return jnp.dot(a.astype(BF16), b, preferred_element_type=F32)

    def local(h):
        gi = GDN_HEADS + h
        beta = beta_all[:, h:h + 1]
        egc = eg_all[:, gi:gi + 1]
        q = _l2norm(conv(h * GDN_DK)) * (GDN_DK ** -0.5)
        k = _l2norm(conv(GDN_QK + h * GDN_DK))
        v = conv(2 * GDN_QK + h * GDN_DK)
        kb = k * beta
        prod = _dot_nt(jnp.concatenate([kb, q], axis=0), k)
        diff = gcs[:, gi:gi + 1] - gcs_t[gi:gi + 1, :]
        decay = jnp.where(bd_tril, jnp.exp(jnp.where(bd_tril, diff, 0.0)), 0.0)
        return dict(neg_l=jnp.where(bd_strict, -(prod[:rows] * decay), 0.0),
                    a_intra=(prod[rows:] * decay).astype(BF16),
                    rhs=jnp.concatenate([v * beta, kb * egc], axis=1).astype(BF16),
                    qdec=(q * egc).astype(BF16),
                    kdec=(k * ekd_all[:, gi:gi + 1]).astype(BF16))

    for g0 in range(0, GDN_HEADS, GDN_GROUP):
        heads = list(range(g0, g0 + GDN_GROUP))
        loc = [local(h) for h in heads]
        neg_l = [x["neg_l"].astype(BF16) for x in loc]
        p_bd = [n * m_base for n in neg_l]
        p_cat = [to_cat(p) for p in p_bd]
        inv = [eye_cat + p.astype(F32) for p in p_cat]
        p_cat = [mm(p, b) for p, b in zip(p_cat, p_bd)]
        power = 2
        while 2 * power < GDN_BASE:
            res = [mm(jnp.concatenate([p, t], axis=0), to_bd(p)) for p, t in zip(p_cat, inv)]
            p_cat = [r[:GDN_CHUNK] for r in res]
            inv = [t + r[GDN_CHUNK:] for t, r in zip(inv, res)]
            power *= 2
        inv = [t + mm(t, to_bd(p)) for p, t in zip(p_cat, inv)]
        for m_off in m_merge:
            half = [mm(t, n * m_off) for t, n in zip(inv, neg_l)]
            inv = [t + mm(y, to_bd(t)) for t, y in zip(inv, half)]
        uw = [jnp.dot(to_bd(t), x["rhs"], preferred_element_type=F32) for t, x in zip(inv, loc)]
        state = [s_scr[h] for h in heads]
        v_news = [[] for _ in heads]
        q_states = [[] for _ in heads]
        for c in range(GDN_NC):
            r = slice(c * GDN_CHUNK, (c + 1) * GDN_CHUNK)
            last = (c + 1) * GDN_CHUNK - 1
            for j, h in enumerate(heads):
                gi = GDN_HEADS + h
                lhs = jnp.concatenate([uw[j][r, GDN_DK:].astype(BF16), loc[j]["qdec"][r]], axis=0)
                res = jnp.dot(lhs, state[j].astype(BF16), preferred_element_type=F32)
                v_new = uw[j][r, :GDN_DK] - res[:GDN_CHUNK]
                q_states[j].append(res[GDN_CHUNK:])
                v_news[j].append(v_new)
                state[j] = state[j] * eglast_t[gi:gi + 1, last:last + 1] + _dot_tn(loc[j]["kdec"][r], v_new)
        for j, h in enumerate(heads):
            s_scr[h] = state[j]
            o = jnp.concatenate(q_states[j], axis=0) + jnp.dot(
                loc[j]["a_intra"], jnp.concatenate(v_news[j], axis=0).astype(BF16), preferred_element_type=F32)
            cols = slice(h * GDN_DK, (h + 1) * GDN_DK)
            o_ref[:, cols] = _gated_out(o, gate_ref[:, cols], nw_ref[...]).astype(o_ref.dtype)

    @pl.when(step == pl.num_programs(1) - 1)
    def _():
        sfin_ref[...] = s_scr[...]


def _gdn_prompt(qkv, gate, ba, w_conv, alog, dtb, nw, batch, seq):
    m = batch * seq
    steps = seq // GDN_ROWS
    per_halo = GDN_ROWS // HALO
    row_blk = lambda b, c: (b * steps + c, 0)
    return pl.pallas_call(
        _gdn_prompt_kernel,
        grid=(batch, steps),
        in_specs=[pl.BlockSpec((GDN_ROWS, GDN_CONV_CH), row_blk),
                  pl.BlockSpec((HALO, GDN_CONV_CH),
                               lambda b, c: (jnp.maximum((b * steps + c) * per_halo - 1, 0), 0)),
                  pl.BlockSpec((GDN_ROWS, GDN_VW), row_blk),
                  pl.BlockSpec((GDN_ROWS, LANE), row_blk),
                  _resident(w_conv.shape), _resident(alog.shape), _resident(dtb.shape), _resident(nw.shape)],
        out_specs=[pl.BlockSpec((GDN_ROWS, GDN_VW), row_blk),
                   pl.BlockSpec((None, GDN_HEADS, GDN_DK, GDN_DK), lambda b, c: (b, 0, 0, 0))],
        out_shape=[jax.ShapeDtypeStruct((m, GDN_VW), BF16),
                   jax.ShapeDtypeStruct((batch, GDN_HEADS, GDN_DK, GDN_DK), F32)],
        scratch_shapes=[pltpu.VMEM((GDN_HEADS, GDN_DK, GDN_DK), F32),
                        pltpu.VMEM((HALO + GDN_ROWS, GDN_CONV_CH), F32)],
        compiler_params=_cparams(("parallel", "arbitrary")),
        name="gdn_prompt",
    )(qkv, qkv, gate, ba, w_conv, alog, dtb, nw)


def _gdn_decode_prep_kernel(x_ref, hist_ref, ba_ref, wc_ref, alog_ref, dtb_ref, q_ref, k_ref, v_ref, bg_ref):
    for h in range(3 * GDN_HEADS):
        cols = slice(h * GDN_DK, (h + 1) * GDN_DK)
        acc = wc_ref[CONV_WIDTH - 1:CONV_WIDTH, cols] * x_ref[:, cols]
        for i in range(1, CONV_WIDTH):
            acc = acc + wc_ref[CONV_WIDTH - 1 - i:CONV_WIDTH - i, cols] * hist_ref[CONV_WIDTH - 1 - i, :, cols]
        y = _silu(acc)
        if h < GDN_HEADS:
            q_ref[:, cols] = _l2norm(y) * (GDN_DK ** -0.5)
        elif h < 2 * GDN_HEADS:
            k_ref[:, slice((h - GDN_HEADS) * GDN_DK, (h - GDN_HEADS + 1) * GDN_DK)] = _l2norm(y)
        else:
            v_ref[:, slice((h - 2 * GDN_HEADS) * GDN_DK, (h - 2 * GDN_HEADS + 1) * GDN_DK)] = y
    beta, g = _gates(ba_ref[...], alog_ref, dtb_ref)
    lane = lax.broadcasted_iota(jnp.int32, beta.shape, 1)
    bg_ref[...] = jnp.where(lane < GDN_HEADS, beta, g)


def _gdn_decode_prep(x, hist_t, ba, w_conv, alog, dtb):
    nb = x.shape[0]
    head = jax.ShapeDtypeStruct((nb, GDN_QK), F32)
    return pl.pallas_call(
        _gdn_decode_prep_kernel,
        out_shape=[head, head, head, jax.ShapeDtypeStruct((nb, LANE), F32)],
        compiler_params=pltpu.CompilerParams(vmem_limit_bytes=VMEM_LIMIT),
        name="gdn_decode_prep",
    )(x, hist_t, ba, w_conv, alog, dtb)


def _gdn_decode_kernel(aliased, q_ref, k_ref, v_ref, gate_ref, beta_ref, g_ref, s_ref, nw_ref, *refs):
    o_ref, so_ref = refs[1:] if aliased else refs
    pad = jnp.zeros((6, GDN_DK), F32)
    for t in range(q_ref.shape[0]):
        q_all, k_all, v_all = q_ref[t], k_ref[t], v_ref[t]
        beta_all = beta_ref[t]
        eg_all = jnp.exp(g_ref[t])
        qk_all = jnp.sum(q_all * k_all, axis=-1, keepdims=True)
        outs = []
        for h in range(GDN_HEADS):
            s = s_ref[t, h]
            q, k, v = q_all[h:h + 1], k_all[h:h + 1], v_all[h:h + 1]
            beta, eg = beta_all[h:h + 1], eg_all[h:h + 1]
            res = _dot(jnp.concatenate([k, q, pad], axis=0), s)
            v_new = beta * (v - eg * res[0:1])
            o = eg * res[1:2] + qk_all[h:h + 1] * v_new
            k_col = jnp.broadcast_to(k, (GDN_DK, GDN_DK)).T
            so_ref[t, h] = s * eg + k_col * v_new
            outs.append(o)
        o_ref[t] = _gated_out(jnp.concatenate(outs, axis=0), gate_ref[t], nw_ref[...])


def _gdn_decode(layer, q3, k3, v3, gate3, beta3, g3, state, nw, prev):
    nb = q3.shape[0]
    tb = GDN_DECODE_TOKENS
    tok = pl.BlockSpec((tb, GDN_HEADS, GDN_DK), lambda b: (b, 0, 0))
    sc = pl.BlockSpec((tb, GDN_HEADS, 1), lambda b: (b, 0, 0))
    st = pl.BlockSpec((None, tb, GDN_HEADS, GDN_DK, GDN_DK), lambda b: (layer, b, 0, 0, 0))
    in_specs = [tok, tok, tok, tok, sc, sc, st, _resident(nw.shape)]
    args = [q3, k3, v3, gate3, beta3, g3, state, nw]
    aliases = {}
    if prev is not None:
        in_specs.append(pl.BlockSpec(memory_space=pl.ANY))
        aliases = {len(args): 1}
        args.append(prev)
    return pl.pallas_call(
        functools.partial(_gdn_decode_kernel, prev is not None),
        grid=(nb // tb,),
        in_specs=in_specs,
        out_specs=[tok, st],
        out_shape=[jax.ShapeDtypeStruct((nb, GDN_HEADS, GDN_DK), F32), jax.ShapeDtypeStruct(state.shape, F32)],
        input_output_aliases=aliases,
        compiler_params=_cparams(("arbitrary",)),
        name="gdn_decode",
    )(*args)


def _pad_cols(x, start, width=LANE):
    return jnp.pad(x, ((0, 0), (start, width - start - x.shape[1])))


def _positions_minor(x):
    lead = x.shape[:-3]
    n = len(lead)
    xt = x.transpose(*range(n), n + 1, n + 2, n)
    return xt.reshape(*lead, x.shape[-2] * x.shape[-1], x.shape[-3])


def _positions_major(xt, heads):
    lead = xt.shape[:-2]
    n = len(lead)
    x = xt.reshape(*lead, heads, xt.shape[-2] // heads, xt.shape[-1])
    return x.transpose(*range(n), n + 2, n, n + 1)


def kernel(x_prompt, x_sample, state_a_k, state_a_v, state_pool, state_conv, state_gdn, norm_mix, norm_ffn,
           w_in_a, w_out_a, w_pool, pool_scale, w_in_c, w_conv_c, a_log_c, dt_bias_c, norm_out_c, w_out_c,
           w_ffn_gate, w_ffn_up, w_ffn_down, norm_final):
    batch, seq, d = x_prompt.shape
    nb = x_sample.shape[0]
    depth = norm_mix.shape[0]
    tm = 512

    w_in_a_b, w_out_a_b, w_pool_b = w_in_a.astype(BF16), w_out_a.astype(BF16), w_pool.astype(BF16)
    w_in_c_b, w_out_c_b = w_in_c.astype(BF16), w_out_c.astype(BF16)
    wg_b, wu_b, wd_b = w_ffn_gate.astype(BF16), w_ffn_up.astype(BF16), w_ffn_down.astype(BF16)
    w_ba = jnp.stack([_pad_cols(w_in_c_b[i][:, GDN_CONV_CH + GDN_VW:], 0) for i in range(w_in_c.shape[0])])

    def ws_even(i):
        return [(w_in_a_b, (None, d, A_WIDTH), (i, 0, j)) for j in range(4)]

    def ws_odd(i):
        return [(w_in_c_b, (None, d, GDN_CONV_CH), (i, 0, 0)),
                (w_in_c_b, (None, d, GDN_VW), (i, 0, GDN_CONV_CH // GDN_VW)),
                (w_ba, (None, d, LANE), (i, 0, 0))]

    def wos_even(i):
        return [(w_out_a_b, (None, A_WIDTH, d), (i, 0, 0)), (w_out_a_b, (None, POOL_WIDTH, d), (i, 1, 0))]

    def wos_odd(i):
        return [(w_out_c_b, (None, GDN_VW, d), (i, 0, 0))]

    alog_p = _pad_cols(a_log_c, GDN_HEADS)
    dtb_p = _pad_cols(dt_bias_c, GDN_HEADS)
    bias = _attn_bias()
    g_final = norm_final.reshape(1, d)
    n_even = w_in_a.shape[0]

    def ffn_args(layer):
        last = layer == depth - 1
        return (norm_ffn[layer:layer + 1], (wg_b, (None, d, FFN_HIDDEN), (layer, 0, 0)),
                (wu_b, (None, d, FFN_HIDDEN), (layer, 0, 0)), (wd_b, (None, FFN_HIDDEN, d), (layer, 0, 0)),
                g_final if last else None)

    xs = x_sample.reshape(nb, d)
    kt_hist, vt_hist = _positions_minor(state_a_k), _positions_minor(state_a_v)
    kv_out = gdn_out = None
    pool_s, conv_s = [], []
    for layer in range(depth):
        i = layer // 2
        g_mix = norm_mix[layer:layer + 1]
        if layer % 2 == 0:
            q, k, v, u = _norm_proj(xs, g_mix, ws_even(i), nb)
            as_col = lambda a: a.reshape(nb, A_HEADS, HEAD_DIM).transpose(0, 2, 1)
            k_out, v_out, attn = _kv_decode(i, kt_hist, vt_hist, as_col(q), as_col(k), as_col(v), kv_out)
            kv_out = (k_out, v_out)
            pooled = _pool_decode(u, state_pool[i].transpose(1, 0, 2), w_pool_b[i], pool_scale[i:i + 1])
            pool_s.append(jnp.concatenate([state_pool[i][:, 1:], u[:, None, :]], axis=1))
            mixes = [attn.transpose(0, 2, 1).reshape(nb, A_WIDTH), pooled]
            wos = wos_even(i)
        else:
            qkv, gate, ba = _norm_proj(xs, g_mix, ws_odd(i), nb)
            qn, kn, vv, bg = _gdn_decode_prep(qkv, state_conv[i].transpose(1, 0, 2), ba, w_conv_c[i],
                                              alog_p[i:i + 1], dtb_p[i:i + 1])
            shape3 = (nb, GDN_HEADS, GDN_DK)
            o3, gdn_out = _gdn_decode(i, qn.reshape(shape3), kn.reshape(shape3), vv.reshape(shape3),
                                      gate.reshape(shape3), bg[:, :GDN_HEADS].reshape(nb, GDN_HEADS, 1),
                                      bg[:, GDN_HEADS:2 * GDN_HEADS].reshape(nb, GDN_HEADS, 1),
                                      state_gdn, norm_out_c[i:i + 1], gdn_out)
            conv_s.append(jnp.concatenate([state_conv[i][:, 1:], qkv[:, None, :]], axis=1))
            mixes = [o3.reshape(nb, GDN_VW)]
            wos = wos_odd(i)
        xs = _mix_ffn(xs, mixes, wos, *ffn_args(layer), nb)
    y_sample = xs.reshape(nb, 1, d)
    a_k_sample = _positions_major(kv_out[0], A_HEADS)
    a_v_sample = _positions_major(kv_out[1], A_HEADS)

    xp = x_prompt.reshape(batch * seq, d)
    kvt_p = None
    pool_p, conv_p, gdn_p = [], [], []
    for layer in range(depth):
        i = layer // 2
        g_mix = norm_mix[layer:layer + 1]
        if layer % 2 == 0:
            q, k, v, u, kt, vt = _even_in_proj(xp, g_mix, ws_even(i), i, n_even, kvt_p, batch, seq, tm)
            kvt_p = (kt, vt)
            attn, pooled = _even_prompt(q, k, v, u, bias, w_pool_b[i],
                                        pool_scale[i].reshape(len(POOL_WINDOWS), 1, POOL_GROUP_DIM), batch, seq)
            pool_p.append(u.reshape(batch, seq, POOL_WIDTH)[:, seq - POOL_HIST:])
            mixes = [attn, pooled]
            wos = wos_even(i)
        else:
            qkv, gate, ba = _norm_proj(xp, g_mix, ws_odd(i), tm)
            o, s_fin = _gdn_prompt(qkv, gate, ba, w_conv_c[i], alog_p[i:i + 1], dtb_p[i:i + 1],
                                   norm_out_c[i:i + 1], batch, seq)
            conv_p.append(qkv.reshape(batch, seq, GDN_CONV_CH)[:, seq - (CONV_WIDTH - 1):])
            gdn_p.append(s_fin)
            mixes = [o]
            wos = wos_odd(i)
        xp = _mix_ffn(xp, mixes, wos, *ffn_args(layer), tm)
    y_prompt = xp.reshape(batch, seq, d)
    a_k_prompt = _positions_major(kvt_p[0], A_HEADS)
    a_v_prompt = _positions_major(kvt_p[1], A_HEADS)

    return (y_prompt, y_sample, a_k_prompt, a_v_prompt, jnp.stack(pool_p), jnp.stack(conv_p),
            jnp.stack(gdn_p), a_k_sample, a_v_sample, jnp.stack(pool_s), jnp.stack(conv_s), gdn_out)
```
